```python
import math, functools
import jax, jax.numpy as jnp
from jax import lax
import numpy as np

D_MODEL = 1024
BATCH = 16
SEQ = 256
DEPTH = 4
DEC_BATCH = 2
DEC_SEQ = 4096
PAST_LEN = 512

GRID_W = 64
N_MIXERS = 3
N_MLA_LAYERS = (DEPTH + 2) // 3
N_GQA_LAYERS = (DEPTH + 1) // 3
N_DIFF_LAYERS = DEPTH // 3
Q_BLOCK = 128
EPS = 1e-6
ROPE_BASE = 10000.0

MLA_HEADS = 16
MLA_NOPE = 64
MLA_ROPE = 32
MLA_QK = 96
MLA_V = 64
MLA_Q_RANK = 384
MLA_KV_RANK = 256
GQA_Q_HEADS = 16
GQA_KV_HEADS = 4
GQA_GROUP = 4
GQA_HD = 64
DIFF_HEADS = 8
DIFF_HD = 64
MOE_EXPERTS = 16
MOE_GROUPS = 4
EXPERTS_PER_GROUP = 4
MOE_TOPK = 2
MOE_DIM = 256

kernel_name = 'hybrid_mla_gqa_diff_moe_diffusion_step'


def rms_norm(x, g):
    xf = x.astype(jnp.float32)
    y = xf * lax.rsqrt(jnp.mean(xf * xf, axis=-1, keepdims=True) + EPS)
    return (y * g.astype(jnp.float32)).astype(x.dtype)


def axial_rope_tables(n, d_rot):
    rows = n // GRID_W
    row = jnp.repeat(jnp.arange(rows, dtype=jnp.float32), GRID_W)
    col = jnp.tile(jnp.arange(GRID_W, dtype=jnp.float32), rows)
    n_freq = d_rot // 4
    inv = jnp.power(ROPE_BASE, -jnp.arange(n_freq, dtype=jnp.float32) / n_freq)
    ang = jnp.concatenate([row[:, None] * inv, col[:, None] * inv], axis=-1)
    return jnp.cos(ang), jnp.sin(ang)


def apply_rope(x, rope):
    cos, sin = rope
    shp = x.shape
    xr = x.reshape(shp[0], shp[1], -1, shp[-1]).astype(jnp.float32)
    x1, x2 = jnp.split(xr, 2, axis=-1)
    c = cos[None, :, None, :]
    s = sin[None, :, None, :]
    out = jnp.concatenate([x1 * c - x2 * s, x2 * c + x1 * s], axis=-1)
    return out.reshape(shp).astype(x.dtype)


def map_query_blocks(fn, q):
    b, s = q.shape[0], q.shape[1]
    nb = s // Q_BLOCK
    qb = jnp.moveaxis(q.reshape((b, nb, Q_BLOCK) + q.shape[2:]), 1, 0)
    out = lax.map(fn, qb)
    out = jnp.moveaxis(out, 0, 1)
    return out.reshape((b, s) + out.shape[3:])


def attention(q, k, v, scale):
    def block(qb):
        s = jnp.einsum('bqhgd,bkhd->bhgqk', qb, k).astype(jnp.float32) * scale
        p = jax.nn.softmax(s, axis=-1).astype(v.dtype)
        return jnp.einsum('bhgqk,bkhd->bqhgd', p, v)
    return map_query_blocks(block, q)


def diff_attention(q, k, v, lam, scale):
    def block(qb):
        s = jnp.einsum('bqhcd,bkhcd->bhcqk', qb, k).astype(jnp.float32) * scale
        p = jax.nn.softmax(s, axis=-1)
        w = (p[:, :, 0] - lam * p[:, :, 1]).astype(v.dtype)
        return jnp.einsum('bhqk,bkhd->bqhd', w, v)
    return map_query_blocks(block, q)


def mla_mixer(h, ctx, rope, w_dq, g_q, w_uq, w_dkv, g_kv, w_ukv, g_qn, g_kn, w_o):
    b, s, _ = h.shape
    q = (rms_norm(h @ w_dq, g_q) @ w_uq).reshape(b, s, MLA_HEADS, MLA_QK)
    q = rms_norm(q, g_qn)
    kv = h @ w_dkv
    ckv = rms_norm(kv[..., :MLA_KV_RANK], g_kv)
    kpe = kv[..., MLA_KV_RANK:]

    def expand(ckv_, kpe_):
        t = ckv_.shape[1]
        up = (ckv_ @ w_ukv).reshape(b, t, MLA_HEADS, MLA_NOPE + MLA_V)
        k_pe = jnp.broadcast_to(kpe_[:, :, None, :], (b, t, MLA_HEADS, MLA_ROPE))
        k_ = rms_norm(jnp.concatenate([up[..., :MLA_NOPE], k_pe], axis=-1), g_kn)
        return k_, up[..., MLA_NOPE:]

    k, v = expand(ckv, kpe)
    if ctx is not None:
        q = jnp.concatenate([q[..., :MLA_NOPE], apply_rope(q[..., MLA_NOPE:], rope)], axis=-1)
        k = jnp.concatenate([k[..., :MLA_NOPE], apply_rope(k[..., MLA_NOPE:], rope)], axis=-1)
        k_c, v_c = expand(ctx[0], ctx[1])
        k = jnp.concatenate([k_c, k], axis=1)
        v = jnp.concatenate([v_c, v], axis=1)
    o = attention(q[:, :, :, None, :], k, v, MLA_QK ** -0.5)
    return o.reshape(b, s, MLA_HEADS * MLA_V) @ w_o, (ckv, kpe)


def gqa_mixer(h, ctx, rope, w_qkv, g_qn, g_kn, w_o):
    b, s, _ = h.shape
    qkv = h @ w_qkv
    nq = GQA_Q_HEADS * GQA_HD
    nk = GQA_KV_HEADS * GQA_HD
    q = rms_norm(qkv[..., :nq].reshape(b, s, GQA_KV_HEADS, GQA_GROUP, GQA_HD), g_qn)
    k = rms_norm(qkv[..., nq:nq + nk].reshape(b, s, GQA_KV_HEADS, GQA_HD), g_kn)
    v = qkv[..., nq + nk:].reshape(b, s, GQA_KV_HEADS, GQA_HD)
    ctx_kv = (k, v)
    if ctx is not None:
        q = apply_rope(q, rope)
        k = jnp.concatenate([ctx[0], apply_rope(k, rope)], axis=1)
        v = jnp.concatenate([ctx[1], v], axis=1)
    o = attention(q, k, v, GQA_HD ** -0.5)
    return o.reshape(b, s, nq) @ w_o, ctx_kv


def diff_mixer(h, ctx, rope, w_qkv, g_qn, g_kn, lam_q1, lam_k1, lam_q2, lam_k2, g_sub, w_o,
               lam_init=0.8):
    b, s, _ = h.shape
    q, k, v = jnp.split(h @ w_qkv, 3, axis=-1)
    q = rms_norm(q.reshape(b, s, DIFF_HEADS, 2, DIFF_HD), g_qn)
    k = rms_norm(k.reshape(b, s, DIFF_HEADS, 2, DIFF_HD), g_kn)
    v = v.reshape(b, s, DIFF_HEADS, 2 * DIFF_HD)
    ctx_kv = (k, v)
    if ctx is not None:
        q = apply_rope(q, rope)
        k = jnp.concatenate([ctx[0], apply_rope(k, rope)], axis=1)
        v = jnp.concatenate([ctx[1], v], axis=1)
    f32 = jnp.float32
    lam = (jnp.exp(jnp.sum(lam_q1.astype(f32) * lam_k1.astype(f32)))
           - jnp.exp(jnp.sum(lam_q2.astype(f32) * lam_k2.astype(f32))) + lam_init)
    o = diff_attention(q, k, v, lam, DIFF_HD ** -0.5)
    o = rms_norm(o, g_sub) * (1.0 - lam_init)
    return o.reshape(b, s, DIFF_HEADS * 2 * DIFF_HD) @ w_o, ctx_kv


def moe(h, w_router, b_router, w_e1, w_e3, w_e2):
    scores = jax.nn.sigmoid((h @ w_router).astype(jnp.float32))
    sel = scores + b_router.astype(jnp.float32)
    grp = sel.reshape(sel.shape[:-1] + (MOE_GROUPS, EXPERTS_PER_GROUP))
    g_score = jnp.sum(lax.top_k(grp, 2)[0], axis=-1)
    g_sel = jnp.argmax(g_score, axis=-1)
    in_group = (jnp.arange(MOE_EXPERTS) // EXPERTS_PER_GROUP) == g_sel[..., None]
    _, idx = lax.top_k(jnp.where(in_group, sel, -jnp.inf), MOE_TOPK)
    w = jnp.take_along_axis(scores, idx, axis=-1)
    w = w / jnp.sum(w, axis=-1, keepdims=True)
    combine = jnp.sum(jax.nn.one_hot(idx, MOE_EXPERTS, dtype=jnp.float32) * w[..., None], axis=-2)
    a = jnp.einsum('bsd,edf->bsef', h, w_e1)
    u = jnp.einsum('bsd,edf->bsef', h, w_e3)
    hid = jax.nn.silu(a) * u * combine[..., None].astype(h.dtype)
    return jnp.einsum('bsef,efd->bsd', hid, w_e2)


def residual_layer(x, mod, mixer, params, ctx, rope, g_mix, g_ffn, w_router, b_router,
                   w_e1, w_e3, w_e2):
    sh1, sc1, gt1, sh2, sc2, gt2 = jnp.split(mod, 6, axis=-1)
    h = rms_norm(x, g_mix) * (1 + sc1) + sh1
    o, ctx_kv = mixer(h, ctx, rope, *params)
    x = x + gt1 * o
    h = rms_norm(x, g_ffn) * (1 + sc2) + sh2
    x = x + gt2 * moe(h, w_router, b_router, w_e1, w_e3, w_e2)
    return x, ctx_kv


def setup_inputs(seed: int = 0) -> dict:
    key = jax.random.key(seed)
    ks = iter(jax.random.split(key, 64))
    D = D_MODEL

    def nrm(shape, scale=1.0):
        return jax.random.normal(next(ks), shape, jnp.float32) * scale

    def gain(shape):
        return 1.0 + 0.02 * nrm(shape)

    inp = {}
    inp['x_prompt'] = nrm((BATCH, SEQ, D))
    inp['x_sample'] = nrm((DEC_BATCH, DEC_SEQ, D))
    inp['cache_mla_ckv'] = nrm((DEC_BATCH, N_MLA_LAYERS, PAST_LEN, MLA_KV_RANK))
    inp['cache_mla_kpe'] = nrm((DEC_BATCH, N_MLA_LAYERS, PAST_LEN, MLA_ROPE))
    inp['cache_gqa_k'] = nrm((DEC_BATCH, N_GQA_LAYERS, PAST_LEN, GQA_KV_HEADS, GQA_HD))
    inp['cache_gqa_v'] = nrm((DEC_BATCH, N_GQA_LAYERS, PAST_LEN, GQA_KV_HEADS, GQA_HD))
    inp['cache_diff_k'] = nrm((DEC_BATCH, N_DIFF_LAYERS, PAST_LEN, DIFF_HEADS, 2, DIFF_HD))
    inp['cache_diff_v'] = nrm((DEC_BATCH, N_DIFF_LAYERS, PAST_LEN, DIFF_HEADS, 2 * DIFF_HD))
    inp['c'] = nrm((DEC_BATCH, D))
    inp['c_ctx'] = nrm((D,))
    inp['g_mix'] = gain((DEPTH, D))
    inp['g_ffn'] = gain((DEPTH, D))
    inp['w_mod'] = nrm((DEPTH, D, 6 * D), 0.5 * D ** -0.5)
    inp['b_mod'] = nrm((DEPTH, 6 * D), 0.02)
    inp['w_router'] = nrm((D, MOE_EXPERTS), D ** -0.5)
    inp['b_router'] = nrm((MOE_EXPERTS,), 0.01)
    inp['w_e1'] = nrm((DEPTH, MOE_EXPERTS, D, MOE_DIM), D ** -0.5)
    inp['w_e3'] = nrm((DEPTH, MOE_EXPERTS, D, MOE_DIM), D ** -0.5)
    inp['w_e2'] = nrm((DEPTH, MOE_EXPERTS, MOE_DIM, D), MOE_DIM ** -0.5)
    nA = N_MLA_LAYERS
    inp['mla_w_dq'] = nrm((nA, D, MLA_Q_RANK), D ** -0.5)
    inp['mla_g_q'] = gain((nA, MLA_Q_RANK))
    inp['mla_w_uq'] = nrm((nA, MLA_Q_RANK, MLA_HEADS * MLA_QK), MLA_Q_RANK ** -0.5)
    inp['mla_w_dkv'] = nrm((nA, D, MLA_KV_RANK + MLA_ROPE), D ** -0.5)
    inp['mla_g_kv'] = gain((nA, MLA_KV_RANK))
    inp['mla_w_ukv'] = nrm((nA, MLA_KV_RANK, MLA_HEADS * (MLA_NOPE + MLA_V)), MLA_KV_RANK ** -0.5)
    inp['mla_g_qn'] = gain((nA, MLA_QK))
    inp['mla_g_kn'] = gain((nA, MLA_QK))
    inp['mla_w_o'] = nrm((nA, MLA_HEADS * MLA_V, D), (MLA_HEADS * MLA_V) ** -0.5)
    nB = N_GQA_LAYERS
    inp['gqa_w_qkv'] = nrm((nB, D, (GQA_Q_HEADS + 2 * GQA_KV_HEADS) * GQA_HD), D ** -0.5)
    inp['gqa_g_qn'] = gain((nB, GQA_HD))
    inp['gqa_g_kn'] = gain((nB, GQA_HD))
    inp['gqa_w_o'] = nrm((nB, GQA_Q_HEADS * GQA_HD, D), (GQA_Q_HEADS * GQA_HD) ** -0.5)
    nC = N_DIFF_LAYERS
    inp['diff_w_qkv'] = nrm((nC, D, 3 * DIFF_HEADS * 2 * DIFF_HD), D ** -0.5)
    inp['diff_g_qn'] = gain((nC, DIFF_HD))
    inp['diff_g_kn'] = gain((nC, DIFF_HD))
    inp['diff_lam_q1'] = nrm((nC, DIFF_HD), 0.1)
    inp['diff_lam_k1'] = nrm((nC, DIFF_HD), 0.1)
    inp['diff_lam_q2'] = nrm((nC, DIFF_HD), 0.1)
    inp['diff_lam_k2'] = nrm((nC, DIFF_HD), 0.1)
    inp['diff_g_sub'] = gain((nC, 2 * DIFF_HD))
    inp['diff_w_o'] = nrm((nC, DIFF_HEADS * 2 * DIFF_HD, D), (DIFF_HEADS * 2 * DIFF_HD) ** -0.5)
    return inp


def reference(x_prompt, x_sample, cache_mla_ckv, cache_mla_kpe, cache_gqa_k, cache_gqa_v,
              cache_diff_k, cache_diff_v, c, c_ctx, g_mix, g_ffn, w_mod, b_mod, w_router,
              b_router, w_e1, w_e3, w_e2, mla_w_dq, mla_g_q, mla_w_uq, mla_w_dkv, mla_g_kv,
              mla_w_ukv, mla_g_qn, mla_g_kn, mla_w_o, gqa_w_qkv, gqa_g_qn, gqa_g_kn, gqa_w_o,
              diff_w_qkv, diff_g_qn, diff_g_kn, diff_lam_q1, diff_lam_k1, diff_lam_q2,
              diff_lam_k2, diff_g_sub, diff_w_o):
    n_lat = x_sample.shape[1]
    rope_mla = axial_rope_tables(n_lat, MLA_ROPE)
    rope_gqa = axial_rope_tables(n_lat, GQA_HD)
    rope_diff = axial_rope_tables(n_lat, DIFF_HD)
    silu_ctx = jax.nn.silu(c_ctx)
    silu_c = jax.nn.silu(c)
    y_prompt, y_sample = x_prompt, x_sample
    new_mla, new_gqa, new_diff = [], [], []
    for i in range(DEPTH):
        kind, j = i % N_MIXERS, i // N_MIXERS
        if kind == 0:
            mixer = mla_mixer
            params = (mla_w_dq[j], mla_g_q[j], mla_w_uq[j], mla_w_dkv[j], mla_g_kv[j],
                      mla_w_ukv[j], mla_g_qn[j], mla_g_kn[j], mla_w_o[j])
            cache = (cache_mla_ckv[:, j], cache_mla_kpe[:, j])
            rope, store = rope_mla, new_mla
        elif kind == 1:
            mixer = gqa_mixer
            params = (gqa_w_qkv[j], gqa_g_qn[j], gqa_g_kn[j], gqa_w_o[j])
            cache = (cache_gqa_k[:, j], cache_gqa_v[:, j])
            rope, store = rope_gqa, new_gqa
        else:
            mixer = functools.partial(diff_mixer, lam_init=0.8 - 0.6 * math.exp(-0.3 * i))
            params = (diff_w_qkv[j], diff_g_qn[j], diff_g_kn[j], diff_lam_q1[j], diff_lam_k1[j],
                      diff_lam_q2[j], diff_lam_k2[j], diff_g_sub[j], diff_w_o[j])
            cache = (cache_diff_k[:, j], cache_diff_v[:, j])
            rope, store = rope_diff, new_diff
        m_ctx = (silu_ctx @ w_mod[i] + b_mod[i])[None, None, :]
        m_lat = (silu_c @ w_mod[i] + b_mod[i])[:, None, :]
        y_prompt, ctx_kv = residual_layer(y_prompt, m_ctx, mixer, params, None, None,
                                          g_mix[i], g_ffn[i], w_router, b_router,
                                          w_e1[i], w_e3[i], w_e2[i])
        y_sample, _ = residual_layer(y_sample, m_lat, mixer, params, cache, rope,
                                     g_mix[i], g_ffn[i], w_router, b_router,
                                     w_e1[i], w_e3[i], w_e2[i])
        store.append(ctx_kv)
    new_mla_ckv = jnp.stack([t[0] for t in new_mla], axis=1)
    new_mla_kpe = jnp.stack([t[1] for t in new_mla], axis=1)
    new_gqa_k = jnp.stack([t[0] for t in new_gqa], axis=1)
    new_gqa_v = jnp.stack([t[1] for t in new_gqa], axis=1)
    new_diff_k = jnp.stack([t[0] for t in new_diff], axis=1)
    new_diff_v = jnp.stack([t[1] for t in new_diff], axis=1)
    return (y_prompt, y_sample, new_mla_ckv, new_mla_kpe, new_gqa_k, new_gqa_v, new_diff_k, new_diff_v)
```

```python
import functools
import math

import jax
import jax.numpy as jnp
from jax import lax
from jax.experimental import pallas as pl
from jax.experimental.pallas import tpu as pltpu

F32 = jnp.float32
BF16 = jnp.bfloat16

D_MODEL = 1024
BATCH, SEQ = 16, 256
DEC_BATCH, DEC_SEQ = 2, 4096
PAST_LEN = 512
DEPTH = 4
GRID_W = 64
EPS = 1e-6
ROPE_BASE = 10000.0
MLA_HEADS, MLA_NOPE, MLA_ROPE, MLA_QK, MLA_V = 16, 64, 32, 96, 64
MLA_Q_RANK, MLA_KV_RANK = 384, 256
GQA_Q_HEADS, GQA_KV_HEADS, GQA_HD = 16, 4, 64
DIFF_HEADS, DIFF_HD = 8, 64
MOE_EXPERTS, MOE_GROUPS, EXPERTS_PER_GROUP, MOE_DIM = 16, 4, 4, 256

T_PROMPT = BATCH * SEQ
T_SAMPLE = DEC_BATCH * DEC_SEQ
T_ALL = T_PROMPT + T_SAMPLE

LANES = 128
VMEM_LIMIT_BYTES = 56 * 1024 * 1024

TM_PROJ = 256
TM_MOE = 512
TQ_SAMPLE = 512
CK = 512
MOD_NT = 1536
MOD_ROWS = 8


def _cparams(*sem):
    return pltpu.CompilerParams(dimension_semantics=sem, vmem_limit_bytes=VMEM_LIMIT_BYTES)


def _dot(a, b):
    return jnp.dot(a, b, preferred_element_type=F32)


def _dot_nt(a, b):
    return lax.dot_general(a, b, (((1,), (1,)), ((), ())), preferred_element_type=F32)


def _rms(x, g):
    ms = jnp.mean(x * x, axis=-1, keepdims=True)
    return x * lax.rsqrt(ms + EPS) * g


def _norm_mod(x, g, scale, shift):
    return _rms(x, g) * (1.0 + scale) + shift


def _head_norm(blk, g, d_real):
    ms = jnp.sum(blk * blk, axis=-1, keepdims=True) * (1.0 / d_real)
    return blk * lax.rsqrt(ms + EPS) * g


def _rope(x, c, sa, sb, half):
    return x * c + pltpu.roll(x, LANES - half, 1) * sa + pltpu.roll(x, half, 1) * sb


def _silu(x):
    return x / (1.0 + jnp.exp(-x))


def _mod_slices(m):
    d = D_MODEL
    return [m[:, k * d:(k + 1) * d] for k in range(6)]


def _mod_kernel(c_ref, w_ref, b_ref, o_ref):
    s = _silu(c_ref[...]).astype(BF16)
    o_ref[0] = _dot(s, w_ref[0].astype(BF16)) + b_ref[0]


def _modulation(c_all, w_mod, b_mod):
    n = 6 * D_MODEL
    return pl.pallas_call(
        _mod_kernel,
        out_shape=jax.ShapeDtypeStruct((DEPTH, MOD_ROWS, n), F32),
        grid=(DEPTH, n // MOD_NT),
        in_specs=[
            pl.BlockSpec((MOD_ROWS, D_MODEL), lambda l, j: (0, 0)),
            pl.BlockSpec((1, D_MODEL, MOD_NT), lambda l, j: (l, 0, j)),
            pl.BlockSpec((1, 1, MOD_NT), lambda l, j: (l, 0, j)),
        ],
        out_specs=pl.BlockSpec((1, MOD_ROWS, MOD_NT), lambda l, j: (l, 0, j)),
        compiler_params=_cparams("arbitrary", "arbitrary"),
        name="modulation",
    )(c_all, w_mod, b_mod.reshape(DEPTH, 1, n))


def _mod_row(i, tm, layer):
    r = jnp.where(i * tm < T_PROMPT, 0, 1 + (i * tm - T_PROMPT) // DEC_SEQ)
    return layer * MOD_ROWS + r


def _table_blk(i, tm):
    npt = T_PROMPT // tm
    return jnp.where(i < npt, i, npt + (i - npt) % (DEC_SEQ // tm))


def _cache_blk(i, tm):
    return jnp.minimum(i, T_PROMPT // tm)


def _qkv_kernel(x_ref, mod_ref, g_ref, w_ref, gq_ref, gk_ref, c_ref, sa_ref, sb_ref,
                q_ref, k_ref, v_ref, kc_ref, vc_ref, *, nq, nk, d_real, half):
    sh, sc = _mod_slices(mod_ref[0])[:2]
    h = _norm_mod(x_ref[...], g_ref[...], sc, sh).astype(BF16)
    y = _dot(h, w_ref[...])
    c, sa, sb = c_ref[...], sa_ref[...], sb_ref[...]
    gq, gk = gq_ref[...], gk_ref[...]
    for j in range(nq):
        blk = _head_norm(y[:, j * LANES:(j + 1) * LANES], gq, d_real)
        q_ref[:, j * LANES:(j + 1) * LANES] = _rope(blk, c, sa, sb, half).astype(BF16)
    for j in range(nk):
        blk = _head_norm(y[:, (nq + j) * LANES:(nq + j + 1) * LANES], gk, d_real)
        kc_ref[:, j * LANES:(j + 1) * LANES] = blk
        k_ref[:, j * LANES:(j + 1) * LANES] = _rope(blk, c, sa, sb, half).astype(BF16)
    v = y[:, (nq + nk) * LANES:]
    vc_ref[...] = v
    v_ref[...] = v.astype(BF16)


def _qkv_proj(x, mod, g, w, gq, gk, tables, layer, *, nq, nk, nv, d_real, half):
    tm = TM_PROJ
    n = w.shape[1]
    wq, wk = nq * LANES, nk * LANES
    row = lambda i: (i, 0)
    const = lambda i: (0, 0)
    tab = pl.BlockSpec((tm, LANES), lambda i: (_table_blk(i, tm), 0))
    cache = lambda width: pl.BlockSpec((tm, width), lambda i: (_cache_blk(i, tm), 0))
    return pl.pallas_call(
        functools.partial(_qkv_kernel, nq=nq, nk=nk, d_real=d_real, half=half),
        out_shape=(
            jax.ShapeDtypeStruct((T_ALL, wq), BF16),
            jax.ShapeDtypeStruct((T_ALL, wk), BF16),
            jax.ShapeDtypeStruct((T_ALL, nv), BF16),
            jax.ShapeDtypeStruct((T_PROMPT + tm, wk), F32),
            jax.ShapeDtypeStruct((T_PROMPT + tm, nv), F32),
        ),
        grid=(T_ALL // tm,),
        in_specs=[
            pl.BlockSpec((tm, D_MODEL), row),
            pl.BlockSpec((1, 1, 6 * D_MODEL), lambda i: (_mod_row(i, tm, layer), 0, 0)),
            pl.BlockSpec((1, D_MODEL), const),
            pl.BlockSpec((D_MODEL, n), const),
            pl.BlockSpec((1, LANES), const),
            pl.BlockSpec((1, LANES), const),
            tab, tab, tab,
        ],
        out_specs=(
            pl.BlockSpec((tm, wq), row),
            pl.BlockSpec((tm, wk), row),
            pl.BlockSpec((tm, nv), row),
            cache(wk),
            cache(nv),
        ),
        compiler_params=_cparams("arbitrary"),
        name="qkv_proj",
    )(x, mod, g, w, gq, gk, *tables)


def _down_kernel(x_ref, mod_ref, g_ref, w_ref, y_ref):
    sh, sc = _mod_slices(mod_ref[0])[:2]
    h = _norm_mod(x_ref[...], g_ref[...], sc, sh).astype(BF16)
    y_ref[...] = _dot(h, w_ref[...])


def _mla_down(x, mod, g, w, layer):
    tm = TM_PROJ
    n = w.shape[1]
    return pl.pallas_call(
        _down_kernel,
        out_shape=jax.ShapeDtypeStruct((T_ALL, n), F32),
        grid=(T_ALL // tm,),
        in_specs=[
            pl.BlockSpec((tm, D_MODEL), lambda i: (i, 0)),
            pl.BlockSpec((1, 1, 6 * D_MODEL), lambda i: (_mod_row(i, tm, layer), 0, 0)),
            pl.BlockSpec((1, D_MODEL), lambda i: (0, 0)),
            pl.BlockSpec((D_MODEL, n), lambda i: (0, 0)),
        ],
        out_specs=pl.BlockSpec((tm, n), lambda i: (i, 0)),
        compiler_params=_cparams("arbitrary"),
        name="mla_down",
    )(x, mod, g, w)


def _mla_expand_kv(ckvn, kpe_blk, wk_ref, wv_ref, gkn):
    ckv_b = ckvn.astype(BF16)
    kcat = jnp.concatenate([ckv_b, kpe_blk.astype(BF16)], axis=-1)
    kk = _dot(kcat, wk_ref[...])
    keys = [_head_norm(kk[:, j * LANES:(j + 1) * LANES], gkn, MLA_QK) for j in range(MLA_HEADS)]
    return keys, _dot(ckv_b, wv_ref[...])


def _mla_up_kernel(y_ref, gq_ref, wuq_ref, gqn_ref, gkv_ref, wk_ref, wv_ref, gkn_ref,
                   c_ref, sa_ref, sb_ref, q_ref, k_ref, v_ref, ckv_ref, kpe_ref):
    half = MLA_ROPE // 2
    y = y_ref[...]
    c, sa, sb = c_ref[...], sa_ref[...], sb_ref[...]
    cqn = _rms(y[:, :MLA_Q_RANK], gq_ref[...]).astype(BF16)
    q = _dot(cqn, wuq_ref[...])
    gqn = gqn_ref[...]
    for j in range(MLA_HEADS):
        blk = _head_norm(q[:, j * LANES:(j + 1) * LANES], gqn, MLA_QK)
        q_ref[:, j * LANES:(j + 1) * LANES] = _rope(blk, c, sa, sb, half).astype(BF16)
    ckvn = _rms(y[:, MLA_Q_RANK:MLA_Q_RANK + MLA_KV_RANK], gkv_ref[...])
    kpe_blk = y[:, MLA_Q_RANK + MLA_KV_RANK:]
    ckv_ref[...] = ckvn
    kpe_ref[...] = kpe_blk
    keys, v = _mla_expand_kv(ckvn, kpe_blk, wk_ref, wv_ref, gkn_ref[...])
    for j in range(MLA_HEADS):
        k_ref[:, j * LANES:(j + 1) * LANES] = _rope(keys[j], c, sa, sb, half).astype(BF16)
    v_ref[...] = v.astype(BF16)


def _mla_up(y1, gq, wuq, gqn, gkv, wk, wv, gkn, tables):
    tm = TM_PROJ
    hq = MLA_HEADS * LANES
    hv = MLA_HEADS * MLA_V
    row = lambda i: (i, 0)
    const = lambda i: (0, 0)
    tab = pl.BlockSpec((tm, LANES), lambda i: (_table_blk(i, tm), 0))
    cache = lambda width: pl.BlockSpec((tm, width), lambda i: (_cache_blk(i, tm), 0))
    return pl.pallas_call(
        _mla_up_kernel,
        out_shape=(
            jax.ShapeDtypeStruct((T_ALL, hq), BF16),
            jax.ShapeDtypeStruct((T_ALL, hq), BF16),
            jax.ShapeDtypeStruct((T_ALL, hv), BF16),
            jax.ShapeDtypeStruct((T_PROMPT + tm, MLA_KV_RANK), F32),
            jax.ShapeDtypeStruct((T_PROMPT + tm, LANES), F32),
        ),
        grid=(T_ALL // tm,),
        in_specs=[
            pl.BlockSpec((tm, y1.shape[1]), row),
            pl.BlockSpec((1, MLA_Q_RANK), const),
            pl.BlockSpec(wuq.shape, const),
            pl.BlockSpec((1, LANES), const),
            pl.BlockSpec((1, MLA_KV_RANK), const),
            pl.BlockSpec(wk.shape, const),
            pl.BlockSpec(wv.shape, const),
            pl.BlockSpec((1, LANES), const),
            tab, tab, tab,
        ],
        out_specs=(
            pl.BlockSpec((tm, hq), row),
            pl.BlockSpec((tm, hq), row),
            pl.BlockSpec((tm, hv), row),
            cache(MLA_KV_RANK),
            cache(LANES),
        ),
        compiler_params=_cparams("arbitrary"),
        name="mla_up",
    )(y1, gq, wuq, gqn, gkv, wk, wv, gkn, *tables)


def _mla_cache_kernel(ckv_ref, kpe_ref, wk_ref, wv_ref, gkn_ref, k_ref, v_ref):
    keys, v = _mla_expand_kv(ckv_ref[...], kpe_ref[...], wk_ref, wv_ref, gkn_ref[...])
    for j in range(MLA_HEADS):
        k_ref[:, j * LANES:(j + 1) * LANES] = keys[j].astype(BF16)
    v_ref[...] = v.astype(BF16)


def _mla_cache_expand(ckv, kpe_blk, wk, wv, gkn):
    rows = ckv.shape[0]
    tm = TM_PROJ
    hq = MLA_HEADS * LANES
    hv = MLA_HEADS * MLA_V
    row = lambda i: (i, 0)
    const = lambda i: (0, 0)
    return pl.pallas_call(
        _mla_cache_kernel,
        out_shape=(jax.ShapeDtypeStruct((rows, hq), BF16), jax.ShapeDtypeStruct((rows, hv), BF16)),
        grid=(rows // tm,),
        in_specs=[
            pl.BlockSpec((tm, MLA_KV_RANK), row),
            pl.BlockSpec((tm, LANES), row),
            pl.BlockSpec(wk.shape, const),
            pl.BlockSpec(wv.shape, const),
            pl.BlockSpec((1, LANES), const),
        ],
        out_specs=(pl.BlockSpec((tm, hq), row), pl.BlockSpec((tm, hv), row)),
        compiler_params=_cparams("arbitrary"),
        name="mla_cache_expand",
    )(ckv, kpe_blk, wk, wv, gkn)


def _attend(q, k, v, state):
    s = _dot_nt(q, k)
    m_cur = jnp.max(s, axis=-1, keepdims=True)
    if state is None:
        p = jnp.exp(s - m_cur)
        return m_cur, jnp.sum(p, axis=-1, keepdims=True), _dot(p.astype(BF16), v)
    m, l, acc = state
    m_new = jnp.maximum(m, m_cur)
    alpha = jnp.exp(m - m_new)
    p = jnp.exp(s - m_new)
    l = alpha * l + jnp.sum(p, axis=-1, keepdims=True)
    return m_new, l, alpha * acc + _dot(p.astype(BF16), v)


def _attn_kernel(*refs, mode, n_blocks, k_shared, has_cache, n_new, lam_init):
    refs = list(refs)
    q_ref, k_ref, v_ref = refs[:3]
    pos = 3
    if has_cache:
        kc_ref, vc_ref = refs[pos:pos + 2]
        pos += 2
    if mode == "diff":
        lam_ref, gsub_ref = refs[pos:pos + 2]
        pos += 2
        lp = lam_ref[...]
        lam = (jnp.exp(jnp.sum(lp[0:1] * lp[1:2], axis=-1, keepdims=True))
               - jnp.exp(jnp.sum(lp[2:3] * lp[3:4], axis=-1, keepdims=True)) + lam_init)
    o_ref = refs[pos]

    for jb in range(n_blocks):
        qa = q_ref[:, (2 * jb) * LANES:(2 * jb + 1) * LANES]
        qb = q_ref[:, (2 * jb + 1) * LANES:(2 * jb + 2) * LANES]
        if k_shared:
            ka_sl = kb_sl = slice((jb // 2) * LANES, (jb // 2 + 1) * LANES)
            v_sl = ka_sl
        else:
            ka_sl = slice((2 * jb) * LANES, (2 * jb + 1) * LANES)
            kb_sl = slice((2 * jb + 1) * LANES, (2 * jb + 2) * LANES)
            v_sl = slice(jb * LANES, (jb + 1) * LANES)

        if has_cache:
            vcache = vc_ref[:, v_sl]
            sa = _attend(qa, kc_ref[:, ka_sl], vcache, None)
            sb = _attend(qb, kc_ref[:, kb_sl], vcache, None)

            def body(c, carry):
                r = pl.ds(pl.multiple_of(c * CK, CK), CK)
                vv = v_ref[r, v_sl]
                return (_attend(qa, k_ref[r, ka_sl], vv, carry[0]),
                        _attend(qb, k_ref[r, kb_sl], vv, carry[1]))

            sa, sb = lax.fori_loop(0, n_new, body, (sa, sb))
        else:
            vv = v_ref[:, v_sl]
            sa = _attend(qa, k_ref[:, ka_sl], vv, None)
            sb = _attend(qb, k_ref[:, kb_sl], vv, None)

        oa = sa[2] * (1.0 / sa[1])
        ob = sb[2] * (1.0 / sb[1])
        if mode == "pair":
            lane = lax.broadcasted_iota(jnp.int32, oa.shape, 1)
            o = jnp.where(lane < LANES // 2, oa, ob)
        else:
            o = oa - lam * ob
            o = _rms(o, gsub_ref[...]) * (1.0 - lam_init)
        o_ref[:, jb * LANES:(jb + 1) * LANES] = o.astype(BF16)


def _attention(q, k, v, kc, vc, extra, *, mode, k_shared, lam_init):
    n_blk = D_MODEL // LANES
    kw_all = k.shape[1]
    vw_all = v.shape[1]
    extra_specs_1 = [pl.BlockSpec(e.shape, lambda b: (0, 0)) for e in extra]
    extra_specs_3 = [pl.BlockSpec(e.shape, lambda b, j, i: (0, 0)) for e in extra]

    o_p = pl.pallas_call(
        functools.partial(_attn_kernel, mode=mode, n_blocks=n_blk, k_shared=k_shared,
                          has_cache=False, n_new=0, lam_init=lam_init),
        out_shape=jax.ShapeDtypeStruct((T_PROMPT, D_MODEL), BF16),
        grid=(BATCH,),
        in_specs=[
            pl.BlockSpec((SEQ, q.shape[1]), lambda b: (b, 0)),
            pl.BlockSpec((SEQ, kw_all), lambda b: (b, 0)),
            pl.BlockSpec((SEQ, vw_all), lambda b: (b, 0)),
        ] + extra_specs_1,
        out_specs=pl.BlockSpec((SEQ, D_MODEL), lambda b: (b, 0)),
        compiler_params=_cparams("arbitrary"),
        name="attn_prompt",
    )(q, k, v, *extra)

    tq = TQ_SAMPLE
    q_rows = lambda b, j, i: (T_PROMPT // tq + b * (DEC_SEQ // tq) + i, j)
    kv_col = (lambda j: j // 2) if k_shared else (lambda j: j)
    kw = LANES if k_shared else 2 * LANES
    o_s = pl.pallas_call(
        functools.partial(_attn_kernel, mode=mode, n_blocks=1, k_shared=k_shared,
                          has_cache=True, n_new=DEC_SEQ // CK, lam_init=lam_init),
        out_shape=jax.ShapeDtypeStruct((T_SAMPLE, D_MODEL), BF16),
        grid=(DEC_BATCH, n_blk, DEC_SEQ // tq),
        in_specs=[
            pl.BlockSpec((tq, 2 * LANES), q_rows),
            pl.BlockSpec((DEC_SEQ, kw), lambda b, j, i: (T_PROMPT // DEC_SEQ + b, kv_col(j))),
            pl.BlockSpec((DEC_SEQ, LANES), lambda b, j, i: (T_PROMPT // DEC_SEQ + b, kv_col(j))),
            pl.BlockSpec((PAST_LEN, kw), lambda b, j, i: (b, kv_col(j))),
            pl.BlockSpec((PAST_LEN, LANES), lambda b, j, i: (b, kv_col(j))),
        ] + extra_specs_3,
        out_specs=pl.BlockSpec((tq, LANES), lambda b, j, i: (b * (DEC_SEQ // tq) + i, j)),
        compiler_params=_cparams("arbitrary", "arbitrary", "arbitrary"),
        name="attn_latent",
    )(q, k, v, kc, vc, *extra)
    return o_p, o_s


def _route(sel_t, sc_t, tm):
    sel = [sel_t[e:e + 1, :] for e in range(MOE_EXPERTS)]
    sc = [sc_t[e:e + 1, :] for e in range(MOE_EXPERTS)]
    gscore = []
    for g in range(MOE_GROUPS):
        a, b, c, d = sel[4 * g:4 * g + 4]
        hi1, lo1 = jnp.maximum(a, b), jnp.minimum(a, b)
        hi2, lo2 = jnp.maximum(c, d), jnp.minimum(c, d)
        top1 = jnp.maximum(hi1, hi2)
        top2 = jnp.maximum(jnp.minimum(hi1, hi2), jnp.maximum(lo1, lo2))
        gscore.append(top1 + top2)
    gmax = jnp.maximum(jnp.maximum(gscore[0], gscore[1]), jnp.maximum(gscore[2], gscore[3]))
    taken = jnp.zeros_like(gmax)
    gsel = []
    for g in range(MOE_GROUPS):
        hit = jnp.where(gscore[g] == gmax, 1.0, 0.0) * (1.0 - taken)
        gsel.append(hit)
        taken = taken + hit
    vs, ss = [], []
    for e in range(EXPERTS_PER_GROUP):
        vs.append(sum(gsel[g] * sel[4 * g + e] for g in range(MOE_GROUPS)))
        ss.append(sum(gsel[g] * sc[4 * g + e] for g in range(MOE_GROUPS)))
    ws = []
    for i in range(EXPERTS_PER_GROUP):
        beaten = jnp.zeros_like(gmax)
        for j in range(EXPERTS_PER_GROUP):
            if j < i:
                beaten = beaten + jnp.where(vs[j] >= vs[i], 1.0, 0.0)
            elif j > i:
                beaten = beaten + jnp.where(vs[j] > vs[i], 1.0, 0.0)
        ws.append(jnp.where(beaten < 2.0, ss[i], 0.0))
    denom = (ws[0] + ws[1]) + (ws[2] + ws[3])
    inv = 1.0 / denom
    row = lax.broadcasted_iota(jnp.int32, (MOE_EXPERTS, tm), 0)
    comb = jnp.zeros((MOE_EXPERTS, tm), F32)
    for g in range(MOE_GROUPS):
        for e in range(EXPERTS_PER_GROUP):
            w = gsel[g] * ws[e] * inv
            comb = jnp.where(row == 4 * g + e, jnp.broadcast_to(w, (MOE_EXPERTS, tm)), comb)
    return comb


def _oproj_kernel(op_ref, os_ref, x_ref, mod_ref, wo_ref, g_ref, wr_ref, br_ref,
                  xo_ref, h_ref, comb_ref, *, tm):
    i = pl.program_id(0)
    _, _, gt1, sh2, sc2, _ = _mod_slices(mod_ref[0])
    o = jnp.where(i < T_PROMPT // tm, op_ref[...], os_ref[...])
    xn = x_ref[...] + gt1 * _dot(o, wo_ref[...])
    xo_ref[...] = xn
    h = _norm_mod(xn, g_ref[...], sc2, sh2)
    hi = h.astype(BF16)
    lo = (h - hi.astype(F32)).astype(BF16)
    h_ref[...] = hi
    wr = wr_ref[...]
    r1 = _dot(hi, wr)
    logits = r1[:, :LANES] + r1[:, LANES:] + _dot(lo, wr[:, :LANES])
    scores = 1.0 / (1.0 + jnp.exp(-logits))
    sel = scores + br_ref[...]
    comb = _route(sel.T, scores.T, tm)
    comb = jnp.concatenate([comb, jnp.zeros((LANES - MOE_EXPERTS, tm), F32)], axis=0)
    comb_ref[...] = comb.T


def _oproj(o_p, o_s, x, mod, wo, g_ffn, wr, br, layer):
    tm = TM_PROJ
    npt = T_PROMPT // tm
    row = lambda i: (i, 0)
    const = lambda i: (0, 0)
    return pl.pallas_call(
        functools.partial(_oproj_kernel, tm=tm),
        out_shape=(
            jax.ShapeDtypeStruct((T_ALL, D_MODEL), F32),
            jax.ShapeDtypeStruct((T_ALL, D_MODEL), BF16),
            jax.ShapeDtypeStruct((T_ALL, LANES), F32),
        ),
        grid=(T_ALL // tm,),
        in_specs=[
            pl.BlockSpec((tm, D_MODEL), lambda i: (jnp.minimum(i, npt - 1), 0)),
            pl.BlockSpec((tm, D_MODEL), lambda i: (jnp.maximum(i - npt, 0), 0)),
            pl.BlockSpec((tm, D_MODEL), row),
            pl.BlockSpec((1, 1, 6 * D_MODEL), lambda i: (_mod_row(i, tm, layer), 0, 0)),
            pl.BlockSpec((D_MODEL, D_MODEL), const),
            pl.BlockSpec((1, D_MODEL), const),
            pl.BlockSpec((D_MODEL, 2 * LANES), const),
            pl.BlockSpec((1, LANES), const),
        ],
        out_specs=(
            pl.BlockSpec((tm, D_MODEL), row),
            pl.BlockSpec((tm, D_MODEL), row),
            pl.BlockSpec((tm, LANES), row),
        ),
        compiler_params=_cparams("arbitrary"),
        name="oproj_router",
    )(o_p, o_s, x, mod, wo, g_ffn, wr, br)


def _moe_kernel(h_ref, x_ref, mod_ref, comb_ref, w1_ref, w3_ref, w2_ref, o_ref):
    gt2 = _mod_slices(mod_ref[0])[5]
    h = h_ref[...]
    comb = comb_ref[...]
    acc = jnp.zeros(o_ref.shape, F32)
    for g in range(MOE_GROUPS):
        parts = []
        for e in range(EXPERTS_PER_GROUP):
            ex = EXPERTS_PER_GROUP * g + e
            a = _dot(h, w1_ref[ex])
            u = _dot(h, w3_ref[ex])
            parts.append((_silu(a) * u * comb[:, ex:ex + 1]).astype(BF16))
        hid = jnp.concatenate(parts, axis=-1)
        acc = acc + _dot(hid, w2_ref[g])
    o_ref[...] = x_ref[...] + gt2 * acc


def _moe(h, x, mod, comb, w1, w3, w2, layer):
    tm = TM_MOE
    row = lambda i: (i, 0)
    whole = lambda shape: pl.BlockSpec(shape, lambda i: (0, 0, 0), pipeline_mode=pl.Buffered(1))
    return pl.pallas_call(
        _moe_kernel,
        out_shape=jax.ShapeDtypeStruct((T_ALL, D_MODEL), F32),
        grid=(T_ALL // tm,),
        in_specs=[
            pl.BlockSpec((tm, D_MODEL), row),
            pl.BlockSpec((tm, D_MODEL), row),
            pl.BlockSpec((1, 1, 6 * D_MODEL), lambda i: (_mod_row(i, tm, layer), 0, 0)),
            pl.BlockSpec((tm, LANES), row),
            whole(w1.shape), whole(w3.shape), whole(w2.shape),
        ],
        out_specs=pl.BlockSpec((tm, D_MODEL), row),
        compiler_params=_cparams("arbitrary"),
        name="moe",
    )(h, x, mod, comb, w1, w3, w2)


def _rope_tables(d_rot, lane0):
    n = DEC_SEQ
    half = d_rot // 2
    n_freq = d_rot // 4
    t = jnp.arange(n)
    rowp = (t // GRID_W).astype(F32)
    colp = (t % GRID_W).astype(F32)
    inv = jnp.power(ROPE_BASE, -jnp.arange(n_freq, dtype=F32) / n_freq)
    ang = jnp.concatenate([rowp[:, None] * inv, colp[:, None] * inv], axis=-1)
    cos, sin = jnp.cos(ang), jnp.sin(ang)
    c = jnp.ones((n, LANES), F32).at[:, lane0:lane0 + d_rot].set(jnp.concatenate([cos, cos], -1))
    sa = jnp.zeros((n, LANES), F32).at[:, lane0:lane0 + half].set(-sin)
    sb = jnp.zeros((n, LANES), F32).at[:, lane0 + half:lane0 + d_rot].set(sin)
    ident = (jnp.ones((T_PROMPT, LANES), F32), jnp.zeros((T_PROMPT, LANES), F32),
             jnp.zeros((T_PROMPT, LANES), F32))
    return tuple(jnp.concatenate([i_, t_], axis=0) for i_, t_ in zip(ident, (c, sa, sb)))


def _pad_heads(w, n_heads, d):
    k = w.shape[0]
    w = w.reshape(k, n_heads, d)
    return jnp.pad(w, ((0, 0), (0, 0), (0, LANES - d))).reshape(k, n_heads * LANES)


def _pad_vec(g, scale=1.0):
    return jnp.pad(g.astype(F32) * scale, (0, LANES - g.shape[0])).reshape(1, LANES)


def _pad_rows_to_heads(x, n_heads, d):
    r = x.shape[0]
    x = x.reshape(r, n_heads, d)
    return jnp.pad(x, ((0, 0), (0, 0), (0, LANES - d))).reshape(r, n_heads * LANES)


def kernel(x_prompt, x_sample, cache_mla_ckv, cache_mla_kpe, cache_gqa_k, cache_gqa_v, cache_diff_k, cache_diff_v, c, c_ctx, g_mix, g_ffn, w_mod, b_mod, w_router, b_router, w_e1, w_e3, w_e2, mla_w_dq, mla_g_q, mla_w_uq, mla_w_dkv, mla_g_kv, mla_w_ukv, mla_g_qn, mla_g_kn, mla_w_o, gqa_w_qkv, gqa_g_qn, gqa_g_kn, gqa_w_o, diff_w_qkv, diff_g_qn, diff_g_kn, diff_lam_q1, diff_lam_k1, diff_lam_q2, diff_lam_k2, diff_g_sub, diff_w_o):
    d = D_MODEL
    x = jnp.concatenate([x_prompt.reshape(T_PROMPT, d), x_sample.reshape(T_SAMPLE, d)], axis=0)

    c_all = jnp.concatenate([c_ctx[None, :], c, jnp.zeros((MOD_ROWS - 1 - DEC_BATCH, d), F32)], axis=0)
    mod = _modulation(c_all, w_mod, b_mod).reshape(DEPTH * MOD_ROWS, 1, 6 * d)

    w_hi = w_router.astype(BF16)
    w_lo = (w_router - w_hi.astype(F32)).astype(BF16)
    pad_r = ((0, 0), (0, LANES - MOE_EXPERTS))
    wr = jnp.concatenate([jnp.pad(w_hi, pad_r), jnp.pad(w_lo, pad_r)], axis=1)
    br = _pad_vec(b_router)

    w1b, w3b, w2b = w_e1.astype(BF16), w_e3.astype(BF16), w_e2.astype(BF16)

    tab_mla = _rope_tables(MLA_ROPE, MLA_NOPE)
    tab_64 = _rope_tables(GQA_HD, 0)

    new_mla, new_gqa, new_diff = [], [], []
    for i in range(DEPTH):
        kind, j = i % 3, i // 3
        g_mix_i = g_mix[i].reshape(1, d)
        if kind == 0:
            w_down = jnp.concatenate(
                [mla_w_dq[j], mla_w_dkv[j],
                 jnp.zeros((d, LANES - MLA_ROPE), F32)], axis=1).astype(BF16)
            wuq = _pad_heads(mla_w_uq[j], MLA_HEADS, MLA_QK).astype(BF16)
            ukv = mla_w_ukv[j].reshape(MLA_KV_RANK, MLA_HEADS, MLA_NOPE + MLA_V)
            wk_nope = jnp.pad(ukv[:, :, :MLA_NOPE], ((0, 0), (0, 0), (0, LANES - MLA_NOPE)))
            place = jnp.zeros((LANES, MLA_HEADS, LANES), F32)
            r = jnp.arange(MLA_ROPE)
            place = place.at[r, :, MLA_NOPE + r].set(1.0)
            wk = jnp.concatenate([wk_nope.reshape(MLA_KV_RANK, -1), place.reshape(LANES, -1)],
                                 axis=0).astype(BF16)
            wv = ukv[:, :, MLA_NOPE:].reshape(MLA_KV_RANK, MLA_HEADS * MLA_V).astype(BF16)
            gqn = _pad_vec(mla_g_qn[j], MLA_QK ** -0.5)
            gkn = _pad_vec(mla_g_kn[j])
            y1 = _mla_down(x, mod, g_mix_i, w_down, i)
            q, k, v, ckv_c, kpe_c = _mla_up(
                y1, mla_g_q[j].reshape(1, -1), wuq, gqn, mla_g_kv[j].reshape(1, -1), wk, wv, gkn,
                tab_mla)
            cache_kpe = jnp.pad(cache_mla_kpe[:, j].reshape(DEC_BATCH * PAST_LEN, MLA_ROPE),
                                ((0, 0), (0, LANES - MLA_ROPE)))
            kc, vc = _mla_cache_expand(cache_mla_ckv[:, j].reshape(DEC_BATCH * PAST_LEN, MLA_KV_RANK),
                                       cache_kpe, wk, wv, gkn)
            o_p, o_s = _attention(q, k, v, kc, vc, (), mode="pair", k_shared=False, lam_init=0.0)
            w_o = mla_w_o[j].astype(BF16)
            new_mla.append((ckv_c[:T_PROMPT].reshape(BATCH, SEQ, MLA_KV_RANK),
                            kpe_c[:T_PROMPT, :MLA_ROPE].reshape(BATCH, SEQ, MLA_ROPE)))
        elif kind == 1:
            nq, nk = GQA_Q_HEADS, GQA_KV_HEADS
            wq_, wk_, wv_ = jnp.split(gqa_w_qkv[j], [nq * GQA_HD, (nq + nk) * GQA_HD], axis=1)
            wv_ = wv_.reshape(d, nk, GQA_HD)
            w = jnp.concatenate([_pad_heads(wq_, nq, GQA_HD), _pad_heads(wk_, nk, GQA_HD),
                                 jnp.concatenate([wv_, wv_], axis=-1).reshape(d, nk * LANES)],
                                axis=1).astype(BF16)
            q, k, v, k_c, v_c = _qkv_proj(
                x, mod, g_mix_i, w, _pad_vec(gqa_g_qn[j], GQA_HD ** -0.5), _pad_vec(gqa_g_kn[j]),
                tab_64, i, nq=nq, nk=nk, nv=nk * LANES, d_real=GQA_HD, half=GQA_HD // 2)
            rows = DEC_BATCH * PAST_LEN
            kc = _pad_rows_to_heads(cache_gqa_k[:, j].reshape(rows, nk * GQA_HD), nk, GQA_HD).astype(BF16)
            cv = cache_gqa_v[:, j].reshape(rows, nk, GQA_HD)
            vc = jnp.concatenate([cv, cv], axis=-1).reshape(rows, nk * LANES).astype(BF16)
            o_p, o_s = _attention(q, k, v, kc, vc, (), mode="pair", k_shared=True, lam_init=0.0)
            w_o = gqa_w_o[j].astype(BF16)
            new_gqa.append((k_c[:T_PROMPT].reshape(BATCH, SEQ, nk, LANES)[..., :GQA_HD],
                            v_c[:T_PROMPT].reshape(BATCH, SEQ, nk, LANES)[..., :GQA_HD]))
        else:
            nh = DIFF_HEADS
            lam_init = 0.8 - 0.6 * math.exp(-0.3 * i)
            wq_, wk_, wv_ = jnp.split(diff_w_qkv[j], 3, axis=1)
            w = jnp.concatenate([_pad_heads(wq_, 2 * nh, DIFF_HD), _pad_heads(wk_, 2 * nh, DIFF_HD),
                                 wv_], axis=1).astype(BF16)
            q, k, v, k_c, v_c = _qkv_proj(
                x, mod, g_mix_i, w, _pad_vec(diff_g_qn[j], DIFF_HD ** -0.5), _pad_vec(diff_g_kn[j]),
                tab_64, i, nq=2 * nh, nk=2 * nh, nv=nh * 2 * DIFF_HD, d_real=DIFF_HD,
                half=DIFF_HD // 2)
            rows = DEC_BATCH * PAST_LEN
            kc = _pad_rows_to_heads(cache_diff_k[:, j].reshape(rows, 2 * nh * DIFF_HD), 2 * nh,
                                    DIFF_HD).astype(BF16)
            vc = cache_diff_v[:, j].reshape(rows, nh * 2 * DIFF_HD).astype(BF16)
            lam_p = jnp.concatenate([_pad_vec(diff_lam_q1[j]), _pad_vec(diff_lam_k1[j]),
                                     _pad_vec(diff_lam_q2[j]), _pad_vec(diff_lam_k2[j])], axis=0)
            o_p, o_s = _attention(q, k, v, kc, vc, (lam_p, diff_g_sub[j].reshape(1, LANES)),
                                  mode="diff", k_shared=False, lam_init=lam_init)
            w_o = diff_w_o[j].astype(BF16)
            new_diff.append((k_c[:T_PROMPT].reshape(BATCH, SEQ, nh, 2, LANES)[..., :DIFF_HD],
                             v_c[:T_PROMPT].reshape(BATCH, SEQ, nh, 2 * DIFF_HD)))

        x, h2, comb = _oproj(o_p, o_s, x, mod, w_o, g_ffn[i].reshape(1, d), wr, br, i)
        w2g = w2b[i].reshape(MOE_GROUPS, EXPERTS_PER_GROUP * MOE_DIM, d)
        x = _moe(h2, x, mod, comb, w1b[i], w3b[i], w2g, i)

    y_prompt = x[:T_PROMPT].reshape(BATCH, SEQ, d)
    y_sample = x[T_PROMPT:].reshape(DEC_BATCH, DEC_SEQ, d)
    stack = lambda items, k: jnp.stack([t[k] for t in items], axis=1)
    return (y_prompt, y_sample, stack(new_mla, 0), stack(new_mla, 1), stack(new_gqa, 0),
            stack(new_gqa, 1), stack(new_diff, 0), stack(new_diff, 1))
```

```python
import functools
import math

import jax
import jax.numpy as jnp
from jax import lax
from jax.experimental import pallas as pl
from jax.experimental.pallas import tpu as pltpu

F32 = jnp.float32
BF16 = jnp.bfloat16

D_MODEL = 1024
BATCH, SEQ = 16, 256
DEC_BATCH, DEC_SEQ = 2, 4096
PAST_LEN = 512
DEPTH = 4
GRID_W = 64
EPS = 1e-6
ROPE_BASE = 10000.0
LOG2E = math.log2(math.e)
MLA_HEADS, MLA_NOPE, MLA_ROPE, MLA_QK, MLA_V = 16, 64, 32, 96, 64
MLA_Q_RANK, MLA_KV_RANK = 384, 256
GQA_Q_HEADS, GQA_KV_HEADS, GQA_HD = 16, 4, 64
DIFF_HEADS, DIFF_HD = 8, 64
MOE_EXPERTS, MOE_GROUPS, EXPERTS_PER_GROUP, MOE_DIM = 16, 4, 4, 256

T_PROMPT = BATCH * SEQ
T_SAMPLE = DEC_BATCH * DEC_SEQ
T_ALL = T_PROMPT + T_SAMPLE

LANES = 128
VMEM_LIMIT_BYTES = 56 * 1024 * 1024

TM_PROJ = 256
TM_MOE = 512
TQ_SAMPLE = 512
CK = 512
MOD_NT = 1536
MOD_ROWS = 8


def _cparams(*sem):
    return pltpu.CompilerParams(dimension_semantics=sem, vmem_limit_bytes=VMEM_LIMIT_BYTES)


def _dot(a, b):
    return jnp.dot(a, b, preferred_element_type=F32)


def _rms(x, g):
    ms = jnp.mean(x * x, axis=-1, keepdims=True)
    return x * lax.rsqrt(ms + EPS) * g


def _norm_mod(x, g, scale, shift):
    return _rms(x, g) * (1.0 + scale) + shift


def _head_norm(blk, g, d_real):
    ms = jnp.sum(blk * blk, axis=-1, keepdims=True) * (1.0 / d_real)
    return blk * lax.rsqrt(ms + EPS) * g


def _rope(x, c, sa, sb, half):
    return x * c + pltpu.roll(x, LANES - half, 1) * sa + pltpu.roll(x, half, 1) * sb


def _silu(x):
    return x / (1.0 + jnp.exp(-x))


def _mod_slices(m):
    d = D_MODEL
    return [m[:, k * d:(k + 1) * d] for k in range(6)]


def _mod_kernel(c_ref, w_ref, b_ref, o_ref):
    s = _silu(c_ref[...]).astype(BF16)
    o_ref[0] = _dot(s, w_ref[0].astype(BF16)) + b_ref[0]


def _modulation(c_all, w_mod, b_mod):
    n = 6 * D_MODEL
    return pl.pallas_call(
        _mod_kernel,
        out_shape=jax.ShapeDtypeStruct((DEPTH, MOD_ROWS, n), F32),
        grid=(DEPTH, n // MOD_NT),
        in_specs=[
            pl.BlockSpec((MOD_ROWS, D_MODEL), lambda l, j: (0, 0)),
            pl.BlockSpec((1, D_MODEL, MOD_NT), lambda l, j: (l, 0, j)),
            pl.BlockSpec((1, 1, MOD_NT), lambda l, j: (l, 0, j)),
        ],
        out_specs=pl.BlockSpec((1, MOD_ROWS, MOD_NT), lambda l, j: (l, 0, j)),
        compiler_params=_cparams("arbitrary", "arbitrary"),
        name="modulation",
    )(c_all, w_mod, b_mod.reshape(DEPTH, 1, n))


def _mod_row(i, tm, layer):
    r = jnp.where(i * tm < T_PROMPT, 0, 1 + (i * tm - T_PROMPT) // DEC_SEQ)
    return layer * MOD_ROWS + r


def _table_blk(i, tm):
    npt = T_PROMPT // tm
    return jnp.where(i < npt, i, npt + (i - npt) % (DEC_SEQ // tm))


def _cache_blk(i, tm):
    return jnp.minimum(i, T_PROMPT // tm)


def _qkv_kernel(x_ref, mod_ref, g_ref, w_ref, gq_ref, gk_ref, c_ref, sa_ref, sb_ref,
                qt_ref, k_ref, vt_ref, kc_ref, vc_ref, *, nq, nk, d_real, half):
    sh, sc = _mod_slices(mod_ref[0])[:2]
    h = _norm_mod(x_ref[...], g_ref[...], sc, sh).astype(BF16)
    y = _dot(h, w_ref[...])
    c, sa, sb = c_ref[...], sa_ref[...], sb_ref[...]
    gq, gk = gq_ref[...], gk_ref[...]
    for j in range(nq):
        blk = _head_norm(y[:, j * LANES:(j + 1) * LANES], gq, d_real)
        qt_ref[j * LANES:(j + 1) * LANES, :] = _rope(blk, c, sa, sb, half).T.astype(BF16)
    for j in range(nk):
        blk = _head_norm(y[:, (nq + j) * LANES:(nq + j + 1) * LANES], gk, d_real)
        kc_ref[:, j * LANES:(j + 1) * LANES] = blk
        k_ref[:, j * LANES:(j + 1) * LANES] = _rope(blk, c, sa, sb, half).astype(BF16)
    v = y[:, (nq + nk) * LANES:]
    vc_ref[...] = v
    vt_ref[...] = v.T.astype(BF16)


def _qkv_proj(x, mod, g, w, gq, gk, tables, layer, *, nq, nk, nv, d_real, half):
    tm = TM_PROJ
    n = w.shape[1]
    wq, wk = nq * LANES, nk * LANES
    row = lambda i: (i, 0)
    col = lambda i: (0, i)
    const = lambda i: (0, 0)
    tab = pl.BlockSpec((tm, LANES), lambda i: (_table_blk(i, tm), 0))
    cache = lambda width: pl.BlockSpec((tm, width), lambda i: (_cache_blk(i, tm), 0))
    return pl.pallas_call(
        functools.partial(_qkv_kernel, nq=nq, nk=nk, d_real=d_real, half=half),
        out_shape=(
            jax.ShapeDtypeStruct((wq, T_ALL), BF16),
            jax.ShapeDtypeStruct((T_ALL, wk), BF16),
            jax.ShapeDtypeStruct((nv, T_ALL), BF16),
            jax.ShapeDtypeStruct((T_PROMPT + tm, wk), F32),
            jax.ShapeDtypeStruct((T_PROMPT + tm, nv), F32),
        ),
        grid=(T_ALL // tm,),
        in_specs=[
            pl.BlockSpec((tm, D_MODEL), row),
            pl.BlockSpec((1, 1, 6 * D_MODEL), lambda i: (_mod_row(i, tm, layer), 0, 0)),
            pl.BlockSpec((1, D_MODEL), const),
            pl.BlockSpec((D_MODEL, n), const),
            pl.BlockSpec((1, LANES), const),
            pl.BlockSpec((1, LANES), const),
            tab, tab, tab,
        ],
        out_specs=(
            pl.BlockSpec((wq, tm), col),
            pl.BlockSpec((tm, wk), row),
            pl.BlockSpec((nv, tm), col),
            cache(wk),
            cache(nv),
        ),
        compiler_params=_cparams("arbitrary"),
        name="qkv_proj",
    )(x, mod, g, w, gq, gk, *tables)


def _down_kernel(x_ref, mod_ref, g_ref, w_ref, y_ref):
    sh, sc = _mod_slices(mod_ref[0])[:2]
    h = _norm_mod(x_ref[...], g_ref[...], sc, sh).astype(BF16)
    y_ref[...] = _dot(h, w_ref[...])


def _mla_down(x, mod, g, w, layer):
    tm = TM_PROJ
    n = w.shape[1]
    return pl.pallas_call(
        _down_kernel,
        out_shape=jax.ShapeDtypeStruct((T_ALL, n), F32),
        grid=(T_ALL // tm,),
        in_specs=[
            pl.BlockSpec((tm, D_MODEL), lambda i: (i, 0)),
            pl.BlockSpec((1, 1, 6 * D_MODEL), lambda i: (_mod_row(i, tm, layer), 0, 0)),
            pl.BlockSpec((1, D_MODEL), lambda i: (0, 0)),
            pl.BlockSpec((D_MODEL, n), lambda i: (0, 0)),
        ],
        out_specs=pl.BlockSpec((tm, n), lambda i: (i, 0)),
        compiler_params=_cparams("arbitrary"),
        name="mla_down",
    )(x, mod, g, w)


def _mla_expand_kv(ckvn, kpe_blk, wk_ref, wv_ref, gkn):
    ckv_b = ckvn.astype(BF16)
    kcat = jnp.concatenate([ckv_b, kpe_blk.astype(BF16)], axis=-1)
    kk = _dot(kcat, wk_ref[...])
    keys = [_head_norm(kk[:, j * LANES:(j + 1) * LANES], gkn, MLA_QK) for j in range(MLA_HEADS)]
    return keys, _dot(ckv_b, wv_ref[...])


def _mla_up_kernel(y_ref, gq_ref, wuq_ref, gqn_ref, gkv_ref, wk_ref, wv_ref, gkn_ref,
                   c_ref, sa_ref, sb_ref, qt_ref, k_ref, vt_ref, ckv_ref, kpe_ref):
    half = MLA_ROPE // 2
    y = y_ref[...]
    c, sa, sb = c_ref[...], sa_ref[...], sb_ref[...]
    cqn = _rms(y[:, :MLA_Q_RANK], gq_ref[...]).astype(BF16)
    q = _dot(cqn, wuq_ref[...])
    gqn = gqn_ref[...]
    for j in range(MLA_HEADS):
        blk = _head_norm(q[:, j * LANES:(j + 1) * LANES], gqn, MLA_QK)
        qt_ref[j * LANES:(j + 1) * LANES, :] = _rope(blk, c, sa, sb, half).T.astype(BF16)
    ckvn = _rms(y[:, MLA_Q_RANK:MLA_Q_RANK + MLA_KV_RANK], gkv_ref[...])
    kpe_blk = y[:, MLA_Q_RANK + MLA_KV_RANK:]
    ckv_ref[...] = ckvn
    kpe_ref[...] = kpe_blk
    keys, v = _mla_expand_kv(ckvn, kpe_blk, wk_ref, wv_ref, gkn_ref[...])
    for j in range(MLA_HEADS):
        k_ref[:, j * LANES:(j + 1) * LANES] = _rope(keys[j], c, sa, sb, half).astype(BF16)
    vt_ref[...] = v.T.astype(BF16)


def _mla_up(y1, gq, wuq, gqn, gkv, wk, wv, gkn, tables):
    tm = TM_PROJ
    hq = MLA_HEADS * LANES
    hv = MLA_HEADS * MLA_V
    row = lambda i: (i, 0)
    col = lambda i: (0, i)
    const = lambda i: (0, 0)
    tab = pl.BlockSpec((tm, LANES), lambda i: (_table_blk(i, tm), 0))
    cache = lambda width: pl.BlockSpec((tm, width), lambda i: (_cache_blk(i, tm), 0))
    return pl.pallas_call(
        _mla_up_kernel,
        out_shape=(
            jax.ShapeDtypeStruct((hq, T_ALL), BF16),
            jax.ShapeDtypeStruct((T_ALL, hq), BF16),
            jax.ShapeDtypeStruct((hv, T_ALL), BF16),
            jax.ShapeDtypeStruct((T_PROMPT + tm, MLA_KV_RANK), F32),
            jax.ShapeDtypeStruct((T_PROMPT + tm, LANES), F32),
        ),
        grid=(T_ALL // tm,),
        in_specs=[
            pl.BlockSpec((tm, y1.shape[1]), row),
            pl.BlockSpec((1, MLA_Q_RANK), const),
            pl.BlockSpec(wuq.shape, const),
            pl.BlockSpec((1, LANES), const),
            pl.BlockSpec((1, MLA_KV_RANK), const),
            pl.BlockSpec(wk.shape, const),
            pl.BlockSpec(wv.shape, const),
            pl.BlockSpec((1, LANES), const),
            tab, tab, tab,
        ],
        out_specs=(
            pl.BlockSpec((hq, tm), col),
            pl.BlockSpec((tm, hq), row),
            pl.BlockSpec((hv, tm), col),
            cache(MLA_KV_RANK),
            cache(LANES),
        ),
        compiler_params=_cparams("arbitrary"),
        name="mla_up",
    )(y1, gq, wuq, gqn, gkv, wk, wv, gkn, *tables)


def _mla_cache_kernel(ckv_ref, kpe_ref, wk_ref, wv_ref, gkn_ref, k_ref, vt_ref):
    keys, v = _mla_expand_kv(ckv_ref[...], kpe_ref[...], wk_ref, wv_ref, gkn_ref[...])
    for j in range(MLA_HEADS):
        k_ref[:, j * LANES:(j + 1) * LANES] = keys[j].astype(BF16)
    vt_ref[...] = v.T.astype(BF16)


def _mla_cache_expand(ckv, kpe_blk, wk, wv, gkn):
    rows = ckv.shape[0]
    tm = TM_PROJ
    hq = MLA_HEADS * LANES
    hv = MLA_HEADS * MLA_V
    row = lambda i: (i, 0)
    const = lambda i: (0, 0)
    return pl.pallas_call(
        _mla_cache_kernel,
        out_shape=(jax.ShapeDtypeStruct((rows, hq), BF16), jax.ShapeDtypeStruct((hv, rows), BF16)),
        grid=(rows // tm,),
        in_specs=[
            pl.BlockSpec((tm, MLA_KV_RANK), row),
            pl.BlockSpec((tm, LANES), row),
            pl.BlockSpec(wk.shape, const),
            pl.BlockSpec(wv.shape, const),
            pl.BlockSpec((1, LANES), const),
        ],
        out_specs=(pl.BlockSpec((tm, hq), row), pl.BlockSpec((hv, tm), lambda i: (0, i))),
        compiler_params=_cparams("arbitrary"),
        name="mla_cache_expand",
    )(ckv, kpe_blk, wk, wv, gkn)


def _attend(s, vt, state):
    m_cur = jnp.max(s, axis=0, keepdims=True)
    if state is None:
        p = jnp.exp2(s - m_cur)
        return m_cur, jnp.sum(p, axis=0, keepdims=True), _dot(vt, p.astype(BF16))
    m, l, acc = state
    m_new = jnp.maximum(m, m_cur)
    alpha = jnp.exp2(m - m_new)
    p = jnp.exp2(s - m_new)
    l = alpha * l + jnp.sum(p, axis=0, keepdims=True)
    return m_new, l, alpha * acc + _dot(vt, p.astype(BF16))


def _attn_kernel(*refs, mode, n_blocks, k_shared, has_cache, n_keys, ck, lam_init):
    refs = list(refs)
    qt_ref, k_ref, vt_ref = refs[:3]
    pos = 3
    if has_cache:
        kc_ref, vct_ref = refs[pos:pos + 2]
        pos += 2
    if mode == "diff":
        lam_ref, gsub_ref = refs[pos:pos + 2]
        pos += 2
        lp = lam_ref[...]
        lam = (jnp.exp(jnp.sum(lp[0:1] * lp[1:2], axis=-1, keepdims=True))
               - jnp.exp(jnp.sum(lp[2:3] * lp[3:4], axis=-1, keepdims=True)) + lam_init)
    o_ref = refs[pos]

    for jb in range(n_blocks):
        qa = qt_ref[(2 * jb) * LANES:(2 * jb + 1) * LANES, :]
        qb = qt_ref[(2 * jb + 1) * LANES:(2 * jb + 2) * LANES, :]
        if k_shared:
            ka_sl = kb_sl = slice((jb // 2) * LANES, (jb // 2 + 1) * LANES)
            v_sl = ka_sl
        else:
            ka_sl = slice((2 * jb) * LANES, (2 * jb + 1) * LANES)
            kb_sl = slice((2 * jb + 1) * LANES, (2 * jb + 2) * LANES)
            v_sl = slice(jb * LANES, (jb + 1) * LANES)

        chunks = []
        if has_cache:
            chunks += [(kc_ref, vct_ref, c * ck) for c in range(PAST_LEN // ck)]
        chunks += [(k_ref, vt_ref, c * ck) for c in range(n_keys // ck)]
        def scores(qt, k_sl, c):
            kr, _, start = chunks[c]
            return _dot(kr[start:start + ck, k_sl], qt)

        sa = sb = None
        s_a, s_b = scores(qa, ka_sl, 0), scores(qb, kb_sl, 0)
        for c, (_, vr, start) in enumerate(chunks):
            vv = vr[v_sl, start:start + ck]
            last = c + 1 == len(chunks)
            sa = _attend(s_a, vv, sa)
            s_a = None if last else scores(qa, ka_sl, c + 1)
            sb = _attend(s_b, vv, sb)
            s_b = None if last else scores(qb, kb_sl, c + 1)

        oa = sa[2] * (1.0 / sa[1])
        ob = sb[2] * (1.0 / sb[1])
        if mode == "pair":
            row = lax.broadcasted_iota(jnp.int32, oa.shape, 0)
            o = jnp.where(row < LANES // 2, oa, ob).T
        else:
            o = oa - lam * ob
            ms = jnp.mean(o * o, axis=0, keepdims=True)
            o = (o * lax.rsqrt(ms + EPS)).T * (gsub_ref[...] * (1.0 - lam_init))
        o_ref[:, jb * LANES:(jb + 1) * LANES] = o.astype(BF16)


def _attention(qt, k, vt, kc, vct, extra, *, mode, k_shared, lam_init):
    n_blk = D_MODEL // LANES
    extra_specs_1 = [pl.BlockSpec(e.shape, lambda b: (0, 0)) for e in extra]
    extra_specs_3 = [pl.BlockSpec(e.shape, lambda b, j, i: (0, 0)) for e in extra]

    o_p = pl.pallas_call(
        functools.partial(_attn_kernel, mode=mode, n_blocks=n_blk, k_shared=k_shared,
                          has_cache=False, n_keys=SEQ, ck=SEQ, lam_init=lam_init),
        out_shape=jax.ShapeDtypeStruct((T_PROMPT, D_MODEL), BF16),
        grid=(BATCH,),
        in_specs=[
            pl.BlockSpec((qt.shape[0], SEQ), lambda b: (0, b)),
            pl.BlockSpec((SEQ, k.shape[1]), lambda b: (b, 0)),
            pl.BlockSpec((vt.shape[0], SEQ), lambda b: (0, b)),
        ] + extra_specs_1,
        out_specs=pl.BlockSpec((SEQ, D_MODEL), lambda b: (b, 0)),
        compiler_params=_cparams("arbitrary"),
        name="attn_prompt",
    )(qt, k, vt, *extra)

    tq = TQ_SAMPLE
    kv_col = (lambda j: j // 2) if k_shared else (lambda j: j)
    kw = LANES if k_shared else 2 * LANES
    lat = T_PROMPT // DEC_SEQ
    o_s = pl.pallas_call(
        functools.partial(_attn_kernel, mode=mode, n_blocks=1, k_shared=k_shared,
                          has_cache=True, n_keys=DEC_SEQ, ck=CK, lam_init=lam_init),
        out_shape=jax.ShapeDtypeStruct((T_SAMPLE, D_MODEL), BF16),
        grid=(DEC_BATCH, n_blk, DEC_SEQ // tq),
        in_specs=[
            pl.BlockSpec((2 * LANES, tq),
                         lambda b, j, i: (j, T_PROMPT // tq + b * (DEC_SEQ // tq) + i)),
            pl.BlockSpec((DEC_SEQ, kw), lambda b, j, i: (lat + b, kv_col(j))),
            pl.BlockSpec((LANES, DEC_SEQ), lambda b, j, i: (kv_col(j), lat + b)),
            pl.BlockSpec((PAST_LEN, kw), lambda b, j, i: (b, kv_col(j))),
            pl.BlockSpec((LANES, PAST_LEN), lambda b, j, i: (kv_col(j), b)),
        ] + extra_specs_3,
        out_specs=pl.BlockSpec((tq, LANES), lambda b, j, i: (b * (DEC_SEQ // tq) + i, j)),
        compiler_params=_cparams("arbitrary", "arbitrary", "arbitrary"),
        name="attn_latent",
    )(qt, k, vt, kc, vct, *extra)
    return o_p, o_s


def _route(sel_t, sc_t, tm):
    sel = [sel_t[e:e + 1, :] for e in range(MOE_EXPERTS)]
    sc = [sc_t[e:e + 1, :] for e in range(MOE_EXPERTS)]
    gscore = []
    for g in range(MOE_GROUPS):
        a, b, c, d = sel[4 * g:4 * g + 4]
        hi1, lo1 = jnp.maximum(a, b), jnp.minimum(a, b)
        hi2, lo2 = jnp.maximum(c, d), jnp.minimum(c, d)
        top1 = jnp.maximum(hi1, hi2)
        top2 = jnp.maximum(jnp.minimum(hi1, hi2), jnp.maximum(lo1, lo2))
        gscore.append(top1 + top2)
    gmax = jnp.maximum(jnp.maximum(gscore[0], gscore[1]), jnp.maximum(gscore[2], gscore[3]))
    taken = jnp.zeros_like(gmax)
    gsel = []
    for g in range(MOE_GROUPS):
        hit = jnp.where(gscore[g] == gmax, 1.0, 0.0) * (1.0 - taken)
        gsel.append(hit)
        taken = taken + hit
    vs, ss = [], []
    for e in range(EXPERTS_PER_GROUP):
        vs.append(sum(gsel[g] * sel[4 * g + e] for g in range(MOE_GROUPS)))
        ss.append(sum(gsel[g] * sc[4 * g + e] for g in range(MOE_GROUPS)))
    ws = []
    for i in range(EXPERTS_PER_GROUP):
        beaten = jnp.zeros_like(gmax)
        for j in range(EXPERTS_PER_GROUP):
            if j < i:
                beaten = beaten + jnp.where(vs[j] >= vs[i], 1.0, 0.0)
            elif j > i:
                beaten = beaten + jnp.where(vs[j] > vs[i], 1.0, 0.0)
        ws.append(jnp.where(beaten < 2.0, ss[i], 0.0))
    denom = (ws[0] + ws[1]) + (ws[2] + ws[3])
    inv = 1.0 / denom
    row = lax.broadcasted_iota(jnp.int32, (MOE_EXPERTS, tm), 0)
    comb = jnp.zeros((MOE_EXPERTS, tm), F32)
    for g in range(MOE_GROUPS):
        for e in range(EXPERTS_PER_GROUP):
            w = gsel[g] * ws[e] * inv
            comb = jnp.where(row == 4 * g + e, jnp.broadcast_to(w, (MOE_EXPERTS, tm)), comb)
    return comb


def _oproj_kernel(op_ref, os_ref, x_ref, mod_ref, wo_ref, g_ref, wr_ref, br_ref,
                  xo_ref, h_ref, comb_ref, *, tm):
    i = pl.program_id(0)
    _, _, gt1, sh2, sc2, _ = _mod_slices(mod_ref[0])
    o = jnp.where(i < T_PROMPT // tm, op_ref[...], os_ref[...])
    xn = x_ref[...] + gt1 * _dot(o, wo_ref[...])
    xo_ref[...] = xn
    h = _norm_mod(xn, g_ref[...], sc2, sh2)
    hi = h.astype(BF16)
    lo = (h - hi.astype(F32)).astype(BF16)
    h_ref[...] = hi
    wr = wr_ref[...]
    r1 = _dot(hi, wr)
    logits = r1[:, :LANES] + r1[:, LANES:] + _dot(lo, wr[:, :LANES])
    scores = 1.0 / (1.0 + jnp.exp(-logits))
    sel = scores + br_ref[...]
    comb = _route(sel.T, scores.T, tm)
    comb = jnp.concatenate([comb, jnp.zeros((LANES - MOE_EXPERTS, tm), F32)], axis=0)
    comb_ref[...] = comb.T


def _oproj(o_p, o_s, x, mod, wo, g_ffn, wr, br, layer):
    tm = TM_PROJ
    npt = T_PROMPT // tm
    row = lambda i: (i, 0)
    const = lambda i: (0, 0)
    return pl.pallas_call(
        functools.partial(_oproj_kernel, tm=tm),
        out_shape=(
            jax.ShapeDtypeStruct((T_ALL, D_MODEL), F32),
            jax.ShapeDtypeStruct((T_ALL, D_MODEL), BF16),
            jax.ShapeDtypeStruct((T_ALL, LANES), F32),
        ),
        grid=(T_ALL // tm,),
        in_specs=[
            pl.BlockSpec((tm, D_MODEL), lambda i: (jnp.minimum(i, npt - 1), 0)),
            pl.BlockSpec((tm, D_MODEL), lambda i: (jnp.maximum(i - npt, 0), 0)),
            pl.BlockSpec((tm, D_MODEL), row),
            pl.BlockSpec((1, 1, 6 * D_MODEL), lambda i: (_mod_row(i, tm, layer), 0, 0)),
            pl.BlockSpec((D_MODEL, D_MODEL), const),
            pl.BlockSpec((1, D_MODEL), const),
            pl.BlockSpec((D_MODEL, 2 * LANES), const),
            pl.BlockSpec((1, LANES), const),
        ],
        out_specs=(
            pl.BlockSpec((tm, D_MODEL), row),
            pl.BlockSpec((tm, D_MODEL), row),
            pl.BlockSpec((tm, LANES), row),
        ),
        compiler_params=_cparams("arbitrary"),
        name="oproj_router",
    )(o_p, o_s, x, mod, wo, g_ffn, wr, br)


def _moe_kernel(h_ref, x_ref, mod_ref, comb_ref, w1_ref, w3_ref, w2_ref, o_ref):
    gt2 = _mod_slices(mod_ref[0])[5]
    h = h_ref[...]
    comb = comb_ref[...]
    acc = jnp.zeros(o_ref.shape, F32)
    for g in range(MOE_GROUPS):
        parts = []
        for e in range(EXPERTS_PER_GROUP):
            ex = EXPERTS_PER_GROUP * g + e
            a = _dot(h, w1_ref[ex])
            u = _dot(h, w3_ref[ex])
            parts.append((_silu(a) * u * comb[:, ex:ex + 1]).astype(BF16))
        hid = jnp.concatenate(parts, axis=-1)
        acc = acc + _dot(hid, w2_ref[g])
    o_ref[...] = x_ref[...] + gt2 * acc


def _moe(h, x, mod, comb, w1, w3, w2, layer):
    tm = TM_MOE
    row = lambda i: (i, 0)
    whole = lambda shape: pl.BlockSpec(shape, lambda i: (0, 0, 0), pipeline_mode=pl.Buffered(1))
    return pl.pallas_call(
        _moe_kernel,
        out_shape=jax.ShapeDtypeStruct((T_ALL, D_MODEL), F32),
        grid=(T_ALL // tm,),
        in_specs=[
            pl.BlockSpec((tm, D_MODEL), row),
            pl.BlockSpec((tm, D_MODEL), row),
            pl.BlockSpec((1, 1, 6 * D_MODEL), lambda i: (_mod_row(i, tm, layer), 0, 0)),
            pl.BlockSpec((tm, LANES), row),
            whole(w1.shape), whole(w3.shape), whole(w2.shape),
        ],
        out_specs=pl.BlockSpec((tm, D_MODEL), row),
        compiler_params=_cparams("arbitrary"),
        name="moe",
    )(h, x, mod, comb, w1, w3, w2)


def _rope_tables(d_rot, lane0):
    n = DEC_SEQ
    half = d_rot // 2
    n_freq = d_rot // 4
    t = jnp.arange(n)
    rowp = (t // GRID_W).astype(F32)
    colp = (t % GRID_W).astype(F32)
    inv = jnp.power(ROPE_BASE, -jnp.arange(n_freq, dtype=F32) / n_freq)
    ang = jnp.concatenate([rowp[:, None] * inv, colp[:, None] * inv], axis=-1)
    cos, sin = jnp.cos(ang), jnp.sin(ang)
    c = jnp.ones((n, LANES), F32).at[:, lane0:lane0 + d_rot].set(jnp.concatenate([cos, cos], -1))
    sa = jnp.zeros((n, LANES), F32).at[:, lane0:lane0 + half].set(-sin)
    sb = jnp.zeros((n, LANES), F32).at[:, lane0 + half:lane0 + d_rot].set(sin)
    ident = (jnp.ones((T_PROMPT, LANES), F32), jnp.zeros((T_PROMPT, LANES), F32),
             jnp.zeros((T_PROMPT, LANES), F32))
    return tuple(jnp.concatenate([i_, t_], axis=0) for i_, t_ in zip(ident, (c, sa, sb)))


def _pad_heads(w, n_heads, d):
    k = w.shape[0]
    w = w.reshape(k, n_heads, d)
    return jnp.pad(w, ((0, 0), (0, 0), (0, LANES - d))).reshape(k, n_heads * LANES)


def _pad_vec(g, scale=1.0):
    return jnp.pad(g.astype(F32) * scale, (0, LANES - g.shape[0])).reshape(1, LANES)


def _pad_rows_to_heads(x, n_heads, d):
    r = x.shape[0]
    x = x.reshape(r, n_heads, d)
    return jnp.pad(x, ((0, 0), (0, 0), (0, LANES - d))).reshape(r, n_heads * LANES)


def kernel(x_prompt, x_sample, cache_mla_ckv, cache_mla_kpe, cache_gqa_k, cache_gqa_v, cache_diff_k, cache_diff_v, c, c_ctx, g_mix, g_ffn, w_mod, b_mod, w_router, b_router, w_e1, w_e3, w_e2, mla_w_dq, mla_g_q, mla_w_uq, mla_w_dkv, mla_g_kv, mla_w_ukv, mla_g_qn, mla_g_kn, mla_w_o, gqa_w_qkv, gqa_g_qn, gqa_g_kn, gqa_w_o, diff_w_qkv, diff_g_qn, diff_g_kn, diff_lam_q1, diff_lam_k1, diff_lam_q2, diff_lam_k2, diff_g_sub, diff_w_o):
    d = D_MODEL
    x = jnp.concatenate([x_prompt.reshape(T_PROMPT, d), x_sample.reshape(T_SAMPLE, d)], axis=0)

    c_all = jnp.concatenate([c_ctx[None, :], c, jnp.zeros((MOD_ROWS - 1 - DEC_BATCH, d), F32)], axis=0)
    mod = _modulation(c_all, w_mod, b_mod).reshape(DEPTH * MOD_ROWS, 1, 6 * d)

    w_hi = w_router.astype(BF16)
    w_lo = (w_router - w_hi.astype(F32)).astype(BF16)
    pad_r = ((0, 0), (0, LANES - MOE_EXPERTS))
    wr = jnp.concatenate([jnp.pad(w_hi, pad_r), jnp.pad(w_lo, pad_r)], axis=1)
    br = _pad_vec(b_router)

    w1b, w3b, w2b = w_e1.astype(BF16), w_e3.astype(BF16), w_e2.astype(BF16)

    tab_mla = _rope_tables(MLA_ROPE, MLA_NOPE)
    tab_64 = _rope_tables(GQA_HD, 0)

    new_mla, new_gqa, new_diff = [], [], []
    for i in range(DEPTH):
        kind, j = i % 3, i // 3
        g_mix_i = g_mix[i].reshape(1, d)
        if kind == 0:
            w_down = jnp.concatenate(
                [mla_w_dq[j], mla_w_dkv[j],
                 jnp.zeros((d, LANES - MLA_ROPE), F32)], axis=1).astype(BF16)
            wuq = _pad_heads(mla_w_uq[j], MLA_HEADS, MLA_QK).astype(BF16)
            ukv = mla_w_ukv[j].reshape(MLA_KV_RANK, MLA_HEADS, MLA_NOPE + MLA_V)
            wk_nope = jnp.pad(ukv[:, :, :MLA_NOPE], ((0, 0), (0, 0), (0, LANES - MLA_NOPE)))
            place = jnp.zeros((LANES, MLA_HEADS, LANES), F32)
            r = jnp.arange(MLA_ROPE)
            place = place.at[r, :, MLA_NOPE + r].set(1.0)
            wk = jnp.concatenate([wk_nope.reshape(MLA_KV_RANK, -1), place.reshape(LANES, -1)],
                                 axis=0).astype(BF16)
            wv = ukv[:, :, MLA_NOPE:].reshape(MLA_KV_RANK, MLA_HEADS * MLA_V).astype(BF16)
            gqn = _pad_vec(mla_g_qn[j], LOG2E * MLA_QK ** -0.5)
            gkn = _pad_vec(mla_g_kn[j])
            y1 = _mla_down(x, mod, g_mix_i, w_down, i)
            q, k, v, ckv_c, kpe_c = _mla_up(
                y1, mla_g_q[j].reshape(1, -1), wuq, gqn, mla_g_kv[j].reshape(1, -1), wk, wv, gkn,
                tab_mla)
            cache_kpe = jnp.pad(cache_mla_kpe[:, j].reshape(DEC_BATCH * PAST_LEN, MLA_ROPE),
                                ((0, 0), (0, LANES - MLA_ROPE)))
            kc, vc = _mla_cache_expand(cache_mla_ckv[:, j].reshape(DEC_BATCH * PAST_LEN, MLA_KV_RANK),
                                       cache_kpe, wk, wv, gkn)
            o_p, o_s = _attention(q, k, v, kc, vc, (), mode="pair", k_shared=False, lam_init=0.0)
            w_o = mla_w_o[j].astype(BF16)
            new_mla.append((ckv_c[:T_PROMPT].reshape(BATCH, SEQ, MLA_KV_RANK),
                            kpe_c[:T_PROMPT, :MLA_ROPE].reshape(BATCH, SEQ, MLA_ROPE)))
        elif kind == 1:
            nq, nk = GQA_Q_HEADS, GQA_KV_HEADS
            wq_, wk_, wv_ = jnp.split(gqa_w_qkv[j], [nq * GQA_HD, (nq + nk) * GQA_HD], axis=1)
            wv_ = wv_.reshape(d, nk, GQA_HD)
            w = jnp.concatenate([_pad_heads(wq_, nq, GQA_HD), _pad_heads(wk_, nk, GQA_HD),
                                 jnp.concatenate([wv_, wv_], axis=-1).reshape(d, nk * LANES)],
                                axis=1).astype(BF16)
            q, k, v, k_c, v_c = _qkv_proj(
                x, mod, g_mix_i, w, _pad_vec(gqa_g_qn[j], LOG2E * GQA_HD ** -0.5), _pad_vec(gqa_g_kn[j]),
                tab_64, i, nq=nq, nk=nk, nv=nk * LANES, d_real=GQA_HD, half=GQA_HD // 2)
            rows = DEC_BATCH * PAST_LEN
            kc = _pad_rows_to_heads(cache_gqa_k[:, j].reshape(rows, nk * GQA_HD), nk, GQA_HD).astype(BF16)
            cv = cache_gqa_v[:, j].reshape(rows, nk, GQA_HD)
            vc = jnp.concatenate([cv, cv], axis=-1).reshape(rows, nk * LANES).astype(BF16).T
            o_p, o_s = _attention(q, k, v, kc, vc, (), mode="pair", k_shared=True, lam_init=0.0)
            w_o = gqa_w_o[j].astype(BF16)
            new_gqa.append((k_c[:T_PROMPT].reshape(BATCH, SEQ, nk, LANES)[..., :GQA_HD],
                            v_c[:T_PROMPT].reshape(BATCH, SEQ, nk, LANES)[..., :GQA_HD]))
        else:
            nh = DIFF_HEADS
            lam_init = 0.8 - 0.6 * math.exp(-0.3 * i)
            wq_, wk_, wv_ = jnp.split(diff_w_qkv[j], 3, axis=1)
            w = jnp.concatenate([_pad_heads(wq_, 2 * nh, DIFF_HD), _pad_heads(wk_, 2 * nh, DIFF_HD),
                                 wv_], axis=1).astype(BF16)
            q, k, v, k_c, v_c = _qkv_proj(
                x, mod, g_mix_i, w, _pad_vec(diff_g_qn[j], LOG2E * DIFF_HD ** -0.5), _pad_vec(diff_g_kn[j]),
                tab_64, i, nq=2 * nh, nk=2 * nh, nv=nh * 2 * DIFF_HD, d_real=DIFF_HD,
                half=DIFF_HD // 2)
            rows = DEC_BATCH * PAST_LEN
            kc = _pad_rows_to_heads(cache_diff_k[:, j].reshape(rows, 2 * nh * DIFF_HD), 2 * nh,
                                    DIFF_HD).astype(BF16)
            vc = cache_diff_v[:, j].reshape(rows, nh * 2 * DIFF_HD).astype(BF16).T
            lam_p = jnp.concatenate([_pad_vec(diff_lam_q1[j]), _pad_vec(diff_lam_k1[j]),
                                     _pad_vec(diff_lam_q2[j]), _pad_vec(diff_lam_k2[j])], axis=0)
            o_p, o_s = _attention(q, k, v, kc, vc, (lam_p, diff_g_sub[j].reshape(1, LANES)),
                                  mode="diff", k_shared=False, lam_init=lam_init)
            w_o = diff_w_o[j].astype(BF16)
            new_diff.append((k_c[:T_PROMPT].reshape(BATCH, SEQ, nh, 2, LANES)[..., :DIFF_HD],
                             v_c[:T_PROMPT].reshape(BATCH, SEQ, nh, 2 * DIFF_HD)))

        x, h2, comb = _oproj(o_p, o_s, x, mod, w_o, g_ffn[i].reshape(1, d), wr, br, i)
        w2g = w2b[i].reshape(MOE_GROUPS, EXPERTS_PER_GROUP * MOE_DIM, d)
        x = _moe(h2, x, mod, comb, w1b[i], w3b[i], w2g, i)

    y_prompt = x[:T_PROMPT].reshape(BATCH, SEQ, d)
    y_sample = x[T_PROMPT:].reshape(DEC_BATCH, DEC_SEQ, d)
    stack = lambda items, k: jnp.stack([t[k] for t in items], axis=1)
    return (y_prompt, y_sample, stack(new_mla, 0), stack(new_mla, 1), stack(new_gqa, 0),
            stack(new_gqa, 1), stack(new_diff, 0), stack(new_diff, 1))
```

```python
import functools
import math

import jax
import jax.numpy as jnp
from jax import lax
from jax.experimental import pallas as pl
from jax.experimental.pallas import tpu as pltpu

F32 = jnp.float32
BF16 = jnp.bfloat16

D_MODEL = 1024
BATCH, SEQ = 16, 256
DEC_BATCH, DEC_SEQ = 2, 4096
PAST_LEN = 512
DEPTH = 4
GRID_W = 64
EPS = 1e-6
ROPE_BASE = 10000.0
LOG2E = math.log2(math.e)
MLA_HEADS, MLA_NOPE, MLA_ROPE, MLA_QK, MLA_V = 16, 64, 32, 96, 64
MLA_Q_RANK, MLA_KV_RANK = 384, 256
GQA_Q_HEADS, GQA_KV_HEADS, GQA_HD = 16, 4, 64
DIFF_HEADS, DIFF_HD = 8, 64
MOE_EXPERTS, MOE_GROUPS, EXPERTS_PER_GROUP, MOE_DIM = 16, 4, 4, 256

T_PROMPT = BATCH * SEQ
T_SAMPLE = DEC_BATCH * DEC_SEQ
T_ALL = T_PROMPT + T_SAMPLE

LANES = 128
VMEM_LIMIT_BYTES = 56 * 1024 * 1024

TM_PROJ = 256
TM_MOE = 512
TQ_SAMPLE = 1024
PROMPT_LOOKAHEAD = 4
Q_SPLIT = 4
CK = 512
MOD_NT = 1536
MOD_ROWS = 8


def _cparams(*sem):
    return pltpu.CompilerParams(dimension_semantics=sem, vmem_limit_bytes=VMEM_LIMIT_BYTES)


def _dot(a, b):
    return jnp.dot(a, b, preferred_element_type=F32)


def _rms(x, g):
    ms = jnp.mean(x * x, axis=-1, keepdims=True)
    return x * lax.rsqrt(ms + EPS) * g


def _norm_mod(x, g, scale, shift):
    return _rms(x, g) * (1.0 + scale) + shift


def _silu(x):
    return x / (1.0 + jnp.exp(-x))


def _mod_slices(m):
    d = D_MODEL
    return [m[:, k * d:(k + 1) * d] for k in range(6)]


def _mod_kernel(c_ref, w_ref, b_ref, o_ref):
    s = _silu(c_ref[...]).astype(BF16)
    o_ref[0] = _dot(s, w_ref[0].astype(BF16)) + b_ref[0]


def _modulation(c_all, w_mod, b_mod):
    n = 6 * D_MODEL
    return pl.pallas_call(
        _mod_kernel,
        out_shape=jax.ShapeDtypeStruct((DEPTH, MOD_ROWS, n), F32),
        grid=(DEPTH, n // MOD_NT),
        in_specs=[
            pl.BlockSpec((MOD_ROWS, D_MODEL), lambda l, j: (0, 0)),
            pl.BlockSpec((1, D_MODEL, MOD_NT), lambda l, j: (l, 0, j)),
            pl.BlockSpec((1, 1, MOD_NT), lambda l, j: (l, 0, j)),
        ],
        out_specs=pl.BlockSpec((1, MOD_ROWS, MOD_NT), lambda l, j: (l, 0, j)),
        compiler_params=_cparams("arbitrary", "arbitrary"),
        name="modulation",
    )(c_all, w_mod, b_mod.reshape(DEPTH, 1, n))


def _mod_row(i, tm, layer):
    r = jnp.where(i * tm < T_PROMPT, 0, 1 + (i * tm - T_PROMPT) // DEC_SEQ)
    return layer * MOD_ROWS + r


def _table_blk(i, tm):
    npt = T_PROMPT // tm
    return jnp.where(i < npt, i, npt + (i - npt) % (DEC_SEQ // tm))


def _cache_blk(i, tm):
    return jnp.minimum(i, T_PROMPT // tm)


def _q_epilogue(y, tabs, d_real, half):
    ct, sat, sbt = tabs
    yt = y.T
    ms = jnp.sum(yt * yt, axis=0, keepdims=True) * (1.0 / d_real)
    qn = yt * lax.rsqrt(ms + EPS)
    up = jnp.concatenate([qn[half:], qn[:half]], axis=0)
    dn = jnp.concatenate([qn[LANES - half:], qn[:LANES - half]], axis=0)
    return qn * ct + up * sat + dn * sbt


def _k_epilogue(y, g, tabs, d_real, half):
    ms = jnp.sum(y * y, axis=-1, keepdims=True) * (1.0 / d_real)
    kn = y * lax.rsqrt(ms + EPS)
    kc = kn * g
    if tabs is None:
        return kc, kc
    c, sa, sb = tabs
    return kc, kn * c + pltpu.roll(kn, LANES - half, 1) * sa + pltpu.roll(kn, half, 1) * sb


def _run_segments(segments, epilogue):
    prev = None
    for idx, (lhs, w_ref, col) in enumerate(segments):
        y = _dot(lhs, w_ref[:, col:col + 2 * LANES])
        if prev is not None:
            epilogue(*prev)
        prev = (idx, y)
    epilogue(*prev)


def _store_head_blocks(idx, y, nq, nk, qtabs, ktabs, gk, d_real, half,
                       qt_ref, k_ref, vt_ref, kc_ref, vc_ref):
    for hb in range(2):
        j = 2 * idx + hb
        blk = y[:, hb * LANES:(hb + 1) * LANES]
        if j < nq:
            qt_ref[j * LANES:(j + 1) * LANES, :] = _q_epilogue(blk, qtabs, d_real, half).astype(BF16)
        elif j < nq + nk:
            sl = slice((j - nq) * LANES, (j - nq + 1) * LANES)
            kc, kr = _k_epilogue(blk, gk, ktabs, d_real, half)
            if kc_ref is not None:
                kc_ref[:, sl] = kc
            k_ref[:, sl] = kr.astype(BF16)
        else:
            sl = slice((j - nq - nk) * LANES, (j - nq - nk + 1) * LANES)
            if vc_ref is not None:
                vc_ref[:, sl] = blk
            vt_ref[sl, :] = blk.T.astype(BF16)


def _qkv_kernel(x_ref, mod_ref, g_ref, w_ref, gk_ref, qc_ref, qsa_ref, qsb_ref,
                kc_t_ref, ksa_ref, ksb_ref, qt_ref, k_ref, vt_ref, kc_ref, vc_ref,
                *, nq, nk, d_real, half):
    sh, sc = _mod_slices(mod_ref[0])[:2]
    h = _norm_mod(x_ref[...], g_ref[...], sc, sh).astype(BF16)
    qtabs = (qc_ref[...], qsa_ref[...], qsb_ref[...])
    ktabs = (kc_t_ref[...], ksa_ref[...], ksb_ref[...])
    gk = gk_ref[...]
    epilogue = functools.partial(
        _store_head_blocks, nq=nq, nk=nk, qtabs=qtabs, ktabs=ktabs, gk=gk, d_real=d_real,
        half=half, qt_ref=qt_ref, k_ref=k_ref, vt_ref=vt_ref, kc_ref=kc_ref, vc_ref=vc_ref)
    n_seg = w_ref.shape[1] // (2 * LANES)
    _run_segments([(h, w_ref, s * 2 * LANES) for s in range(n_seg)], epilogue)


def _table_specs(tm):
    qtab = pl.BlockSpec((LANES, tm), lambda i: (0, _table_blk(i, tm)))
    ktab = pl.BlockSpec((tm, LANES), lambda i: (_table_blk(i, tm), 0))
    return [qtab, qtab, qtab, ktab, ktab, ktab]


def _qkv_proj(x, mod, g, w, gk, qtabs, ktabs, layer, *, nq, nk, nv, d_real, half):
    tm = TM_PROJ
    n = w.shape[1]
    wq, wk = nq * LANES, nk * LANES
    row = lambda i: (i, 0)
    col = lambda i: (0, i)
    const = lambda i: (0, 0)
    cache = lambda width: pl.BlockSpec((tm, width), lambda i: (_cache_blk(i, tm), 0))
    return pl.pallas_call(
        functools.partial(_qkv_kernel, nq=nq, nk=nk, d_real=d_real, half=half),
        out_shape=(
            jax.ShapeDtypeStruct((wq, T_ALL), BF16),
            jax.ShapeDtypeStruct((T_ALL, wk), BF16),
            jax.ShapeDtypeStruct((nv, T_ALL), BF16),
            jax.ShapeDtypeStruct((T_PROMPT + tm, wk), F32),
            jax.ShapeDtypeStruct((T_PROMPT + tm, nv), F32),
        ),
        grid=(T_ALL // tm,),
        in_specs=[
            pl.BlockSpec((tm, D_MODEL), row),
            pl.BlockSpec((1, 1, 6 * D_MODEL), lambda i: (_mod_row(i, tm, layer), 0, 0)),
            pl.BlockSpec((1, D_MODEL), const),
            pl.BlockSpec((D_MODEL, n), const),
            pl.BlockSpec((1, LANES), const),
        ] + _table_specs(tm),
        out_specs=(
            pl.BlockSpec((wq, tm), col),
            pl.BlockSpec((tm, wk), row),
            pl.BlockSpec((nv, tm), col),
            cache(wk),
            cache(nv),
        ),
        compiler_params=_cparams("arbitrary"),
        name="qkv_proj",
    )(x, mod, g, w, gk, *qtabs, *ktabs)


def _down_kernel(x_ref, mod_ref, g_ref, w_ref, y_ref):
    sh, sc = _mod_slices(mod_ref[0])[:2]
    h = _norm_mod(x_ref[...], g_ref[...], sc, sh).astype(BF16)
    y_ref[...] = _dot(h, w_ref[...])


def _mla_down(x, mod, g, w, layer):
    tm = TM_PROJ
    n = w.shape[1]
    return pl.pallas_call(
        _down_kernel,
        out_shape=jax.ShapeDtypeStruct((T_ALL, n), F32),
        grid=(T_ALL // tm,),
        in_specs=[
            pl.BlockSpec((tm, D_MODEL), lambda i: (i, 0)),
            pl.BlockSpec((1, 1, 6 * D_MODEL), lambda i: (_mod_row(i, tm, layer), 0, 0)),
            pl.BlockSpec((1, D_MODEL), lambda i: (0, 0)),
            pl.BlockSpec((D_MODEL, n), lambda i: (0, 0)),
        ],
        out_specs=pl.BlockSpec((tm, n), lambda i: (i, 0)),
        compiler_params=_cparams("arbitrary"),
        name="mla_down",
    )(x, mod, g, w)


MLA_NQ = MLA_HEADS
MLA_NV = MLA_HEADS * MLA_V // LANES


def _mla_up_kernel(y_ref, gq_ref, wuq_ref, gkv_ref, wk_ref, wv_ref, gkn_ref,
                   qc_ref, qsa_ref, qsb_ref, kc_t_ref, ksa_ref, ksb_ref,
                   qt_ref, k_ref, vt_ref, ckv_ref, kpe_ref):
    y = y_ref[...]
    cqn = _rms(y[:, :MLA_Q_RANK], gq_ref[...]).astype(BF16)
    ckvn = _rms(y[:, MLA_Q_RANK:MLA_Q_RANK + MLA_KV_RANK], gkv_ref[...])
    kpe_blk = y[:, MLA_Q_RANK + MLA_KV_RANK:]
    ckv_ref[...] = ckvn
    kpe_ref[...] = kpe_blk
    ckv_b = ckvn.astype(BF16)
    kcat = jnp.concatenate([ckv_b, kpe_blk.astype(BF16)], axis=-1)
    qtabs = (qc_ref[...], qsa_ref[...], qsb_ref[...])
    ktabs = (kc_t_ref[...], ksa_ref[...], ksb_ref[...])
    epilogue = functools.partial(
        _store_head_blocks, nq=MLA_NQ, nk=MLA_HEADS, qtabs=qtabs, ktabs=ktabs, gk=gkn_ref[...],
        d_real=MLA_QK, half=MLA_ROPE // 2, qt_ref=qt_ref, k_ref=k_ref, vt_ref=vt_ref,
        kc_ref=None, vc_ref=None)
    seg = 2 * LANES
    segments = ([(cqn, wuq_ref, s * seg) for s in range(MLA_NQ // 2)]
                + [(kcat, wk_ref, s * seg) for s in range(MLA_HEADS // 2)]
                + [(ckv_b, wv_ref, s * seg) for s in range(MLA_NV // 2)])
    _run_segments(segments, epilogue)


def _mla_up(y1, gq, wuq, gkv, wk, wv, gkn, qtabs, ktabs):
    tm = TM_PROJ
    hq = MLA_HEADS * LANES
    hv = MLA_HEADS * MLA_V
    row = lambda i: (i, 0)
    col = lambda i: (0, i)
    const = lambda i: (0, 0)
    cache = lambda width: pl.BlockSpec((tm, width), lambda i: (_cache_blk(i, tm), 0))
    return pl.pallas_call(
        _mla_up_kernel,
        out_shape=(
            jax.ShapeDtypeStruct((hq, T_ALL), BF16),
            jax.ShapeDtypeStruct((T_ALL, hq), BF16),
            jax.ShapeDtypeStruct((hv, T_ALL), BF16),
            jax.ShapeDtypeStruct((T_PROMPT + tm, MLA_KV_RANK), F32),
            jax.ShapeDtypeStruct((T_PROMPT + tm, LANES), F32),
        ),
        grid=(T_ALL // tm,),
        in_specs=[
            pl.BlockSpec((tm, y1.shape[1]), row),
            pl.BlockSpec((1, MLA_Q_RANK), const),
            pl.BlockSpec(wuq.shape, const),
            pl.BlockSpec((1, MLA_KV_RANK), const),
            pl.BlockSpec(wk.shape, const),
            pl.BlockSpec(wv.shape, const),
            pl.BlockSpec((1, LANES), const),
        ] + _table_specs(tm),
        out_specs=(
            pl.BlockSpec((hq, tm), col),
            pl.BlockSpec((tm, hq), row),
            pl.BlockSpec((hv, tm), col),
            cache(MLA_KV_RANK),
            cache(LANES),
        ),
        compiler_params=_cparams("arbitrary"),
        name="mla_up",
    )(y1, gq, wuq, gkv, wk, wv, gkn, *qtabs, *ktabs)


def _mla_cache_kernel(ckv_ref, kpe_ref, wk_ref, wv_ref, gkn_ref, k_ref, vt_ref):
    ckv_b = ckv_ref[...].astype(BF16)
    kcat = jnp.concatenate([ckv_b, kpe_ref[...].astype(BF16)], axis=-1)
    epilogue = functools.partial(
        _store_head_blocks, nq=0, nk=MLA_HEADS, qtabs=None, ktabs=None, gk=gkn_ref[...],
        d_real=MLA_QK, half=MLA_ROPE // 2, qt_ref=None, k_ref=k_ref, vt_ref=vt_ref,
        kc_ref=None, vc_ref=None)
    seg = 2 * LANES
    segments = ([(kcat, wk_ref, s * seg) for s in range(MLA_HEADS // 2)]
                + [(ckv_b, wv_ref, s * seg) for s in range(MLA_NV // 2)])
    _run_segments(segments, epilogue)


def _mla_cache_expand(ckv, kpe_blk, wk, wv, gkn):
    rows = ckv.shape[0]
    tm = TM_PROJ
    hq = MLA_HEADS * LANES
    hv = MLA_HEADS * MLA_V
    row = lambda i: (i, 0)
    const = lambda i: (0, 0)
    return pl.pallas_call(
        _mla_cache_kernel,
        out_shape=(jax.ShapeDtypeStruct((rows, hq), BF16), jax.ShapeDtypeStruct((hv, rows), BF16)),
        grid=(rows // tm,),
        in_specs=[
            pl.BlockSpec((tm, MLA_KV_RANK), row),
            pl.BlockSpec((tm, LANES), row),
            pl.BlockSpec(wk.shape, const),
            pl.BlockSpec(wv.shape, const),
            pl.BlockSpec((1, LANES), const),
        ],
        out_specs=(pl.BlockSpec((tm, hq), row), pl.BlockSpec((hv, tm), lambda i: (0, i))),
        compiler_params=_cparams("arbitrary"),
        name="mla_cache_expand",
    )(ckv, kpe_blk, wk, wv, gkn)


def _attend(s, vt, state):
    m_cur = jnp.max(s, axis=0, keepdims=True)
    if state is None:
        p = jnp.exp2(s - m_cur)
        return m_cur, jnp.sum(p, axis=0, keepdims=True), _dot(vt, p.astype(BF16))
    m, l, acc = state
    m_new = jnp.maximum(m, m_cur)
    alpha = jnp.exp2(m - m_new)
    p = jnp.exp2(s - m_new)
    l = alpha * l + jnp.sum(p, axis=0, keepdims=True)
    return m_new, l, alpha * acc + _dot(vt, p.astype(BF16))


def _attn_kernel(*refs, mode, n_blocks, k_shared, has_cache, n_keys, ck, lam_init):
    refs = list(refs)
    qt_ref, k_ref, vt_ref = refs[:3]
    pos = 3
    if has_cache:
        kc_ref, vct_ref = refs[pos:pos + 2]
        pos += 2
    if mode == "diff":
        lam_ref, gsub_ref = refs[pos:pos + 2]
        pos += 2
        lp = lam_ref[...]
        lam = (jnp.exp(jnp.sum(lp[0:1] * lp[1:2], axis=-1, keepdims=True))
               - jnp.exp(jnp.sum(lp[2:3] * lp[3:4], axis=-1, keepdims=True)) + lam_init)
    o_ref = refs[pos]

    chunks = []
    if has_cache:
        chunks += [(kc_ref, vct_ref, 0, PAST_LEN)]
    chunks += [(k_ref, vt_ref, c * ck, ck) for c in range(n_keys // ck)]

    n_sub = Q_SPLIT if has_cache else 1
    w = qt_ref.shape[1] // n_sub
    streams = []
    for jb in range(n_blocks):
        if k_shared:
            k_sls = [slice((jb // 2) * LANES, (jb // 2 + 1) * LANES)] * 2
            v_sl = k_sls[0]
        else:
            k_sls = [slice((2 * jb + t) * LANES, (2 * jb + t + 1) * LANES) for t in range(2)]
            v_sl = slice(jb * LANES, (jb + 1) * LANES)
        for sub in range(n_sub):
            for t in range(2):
                q_sl = (slice((2 * jb + t) * LANES, (2 * jb + t + 1) * LANES),
                        slice(sub * w, (sub + 1) * w))
                streams.append((q_sl, k_sls[t], v_sl))

    def scores(item):
        x, c = item
        q_sl, k_sl, _ = streams[x]
        kr, _, start, n_ = chunks[c]
        return _dot(kr[start:start + n_, k_sl], qt_ref[q_sl])

    items = [(x, c) for c in range(len(chunks)) for x in range(len(streams))]
    ahead = len(streams) if has_cache else PROMPT_LOOKAHEAD
    sc = {k: scores(items[k]) for k in range(min(ahead, len(items)))}
    st = [None] * len(streams)
    for k, (x, c) in enumerate(items):
        _, vr, start, n_ = chunks[c]
        st[x] = _attend(sc.pop(k), vr[streams[x][2], start:start + n_], st[x])
        if k + ahead < len(items):
            sc[k + ahead] = scores(items[k + ahead])

    for jb in range(n_blocks):
        base = jb * n_sub * 2
        cat = lambda k_, t: jnp.concatenate(
            [st[base + 2 * sub + t][k_] for sub in range(n_sub)], axis=1)
        oa = cat(2, 0) * (1.0 / cat(1, 0))
        ob = cat(2, 1) * (1.0 / cat(1, 1))
        if mode == "pair":
            row = lax.broadcasted_iota(jnp.int32, oa.shape, 0)
            o = jnp.where(row < LANES // 2, oa, ob).T
        else:
            o = oa - lam * ob
            ms = jnp.mean(o * o, axis=0, keepdims=True)
            o = (o * lax.rsqrt(ms + EPS)).T * (gsub_ref[...] * (1.0 - lam_init))
        o_ref[:, jb * LANES:(jb + 1) * LANES] = o.astype(BF16)


def _attention(qt, k, vt, kc, vct, extra, *, mode, k_shared, lam_init):
    n_blk = D_MODEL // LANES
    extra_specs_1 = [pl.BlockSpec(e.shape, lambda b: (0, 0)) for e in extra]
    extra_specs_3 = [pl.BlockSpec(e.shape, lambda b, j, i: (0, 0)) for e in extra]

    o_p = pl.pallas_call(
        functools.partial(_attn_kernel, mode=mode, n_blocks=n_blk, k_shared=k_shared,
                          has_cache=False, n_keys=SEQ, ck=SEQ, lam_init=lam_init),
        out_shape=jax.ShapeDtypeStruct((T_PROMPT, D_MODEL), BF16),
        grid=(BATCH,),
        in_specs=[
            pl.BlockSpec((qt.shape[0], SEQ), lambda b: (0, b)),
            pl.BlockSpec((SEQ, k.shape[1]), lambda b: (b, 0)),
            pl.BlockSpec((vt.shape[0], SEQ), lambda b: (0, b)),
        ] + extra_specs_1,
        out_specs=pl.BlockSpec((SEQ, D_MODEL), lambda b: (b, 0)),
        compiler_params=_cparams("arbitrary"),
        name="attn_prompt",
    )(qt, k, vt, *extra)

    tq = TQ_SAMPLE
    kv_col = (lambda j: j // 2) if k_shared else (lambda j: j)
    kw = LANES if k_shared else 2 * LANES
    lat = T_PROMPT // DEC_SEQ
    o_s = pl.pallas_call(
        functools.partial(_attn_kernel, mode=mode, n_blocks=1, k_shared=k_shared,
                          has_cache=True, n_keys=DEC_SEQ, ck=CK, lam_init=lam_init),
        out_shape=jax.ShapeDtypeStruct((T_SAMPLE, D_MODEL), BF16),
        grid=(DEC_BATCH, n_blk, DEC_SEQ // tq),
        in_specs=[
            pl.BlockSpec((2 * LANES, tq),
                         lambda b, j, i: (j, T_PROMPT // tq + b * (DEC_SEQ // tq) + i)),
            pl.BlockSpec((DEC_SEQ, kw), lambda b, j, i: (lat + b, kv_col(j))),
            pl.BlockSpec((LANES, DEC_SEQ), lambda b, j, i: (kv_col(j), lat + b)),
            pl.BlockSpec((PAST_LEN, kw), lambda b, j, i: (b, kv_col(j))),
            pl.BlockSpec((LANES, PAST_LEN), lambda b, j, i: (kv_col(j), b)),
        ] + extra_specs_3,
        out_specs=pl.BlockSpec((tq, LANES), lambda b, j, i: (b * (DEC_SEQ // tq) + i, j)),
        compiler_params=_cparams("arbitrary", "arbitrary", "arbitrary"),
        name="attn_latent",
    )(qt, k, vt, kc, vct, *extra)
    return o_p, o_s


def _route(sel_t, sc_t, tm):
    sel = [sel_t[e:e + 1, :] for e in range(MOE_EXPERTS)]
    sc = [sc_t[e:e + 1, :] for e in range(MOE_EXPERTS)]
    gscore = []
    for g in range(MOE_GROUPS):
        a, b, c, d = sel[4 * g:4 * g + 4]
        hi1, lo1 = jnp.maximum(a, b), jnp.minimum(a, b)
        hi2, lo2 = jnp.maximum(c, d), jnp.minimum(c, d)
        top1 = jnp.maximum(hi1, hi2)
        top2 = jnp.maximum(jnp.minimum(hi1, hi2), jnp.maximum(lo1, lo2))
        gscore.append(top1 + top2)
    gmax = jnp.maximum(jnp.maximum(gscore[0], gscore[1]), jnp.maximum(gscore[2], gscore[3]))
    taken = jnp.zeros_like(gmax)
    gsel = []
    for g in range(MOE_GROUPS):
        hit = jnp.where(gscore[g] == gmax, 1.0, 0.0) * (1.0 - taken)
        gsel.append(hit)
        taken = taken + hit
    vs, ss = [], []
    for e in range(EXPERTS_PER_GROUP):
        vs.append(sum(gsel[g] * sel[4 * g + e] for g in range(MOE_GROUPS)))
        ss.append(sum(gsel[g] * sc[4 * g + e] for g in range(MOE_GROUPS)))
    ws = []
    for i in range(EXPERTS_PER_GROUP):
        beaten = jnp.zeros_like(gmax)
        for j in range(EXPERTS_PER_GROUP):
            if j < i:
                beaten = beaten + jnp.where(vs[j] >= vs[i], 1.0, 0.0)
            elif j > i:
                beaten = beaten + jnp.where(vs[j] > vs[i], 1.0, 0.0)
        ws.append(jnp.where(beaten < 2.0, ss[i], 0.0))
    denom = (ws[0] + ws[1]) + (ws[2] + ws[3])
    inv = 1.0 / denom
    row = lax.broadcasted_iota(jnp.int32, (MOE_EXPERTS, tm), 0)
    comb = jnp.zeros((MOE_EXPERTS, tm), F32)
    for g in range(MOE_GROUPS):
        for e in range(EXPERTS_PER_GROUP):
            w = gsel[g] * ws[e] * inv
            comb = jnp.where(row == 4 * g + e, jnp.broadcast_to(w, (MOE_EXPERTS, tm)), comb)
    return comb


def _oproj_kernel(op_ref, os_ref, x_ref, mod_ref, wo_ref, g_ref, wr_ref, br_ref,
                  xo_ref, h_ref, comb_ref, *, tm):
    i = pl.program_id(0)
    _, _, gt1, sh2, sc2, _ = _mod_slices(mod_ref[0])
    o = jnp.where(i < T_PROMPT // tm, op_ref[...], os_ref[...])
    xn = x_ref[...] + gt1 * _dot(o, wo_ref[...])
    xo_ref[...] = xn
    h = _norm_mod(xn, g_ref[...], sc2, sh2)
    hi = h.astype(BF16)
    lo = (h - hi.astype(F32)).astype(BF16)
    h_ref[...] = hi
    wr = wr_ref[...]
    r1 = _dot(hi, wr)
    logits = r1[:, :LANES] + r1[:, LANES:] + _dot(lo, wr[:, :LANES])
    scores = 1.0 / (1.0 + jnp.exp(-logits))
    sel = scores + br_ref[...]
    comb = _route(sel.T, scores.T, tm)
    comb = jnp.concatenate([comb, jnp.zeros((LANES - MOE_EXPERTS, tm), F32)], axis=0)
    comb_ref[...] = comb.T


def _oproj(o_p, o_s, x, mod, wo, g_ffn, wr, br, layer):
    tm = TM_PROJ
    npt = T_PROMPT // tm
    row = lambda i: (i, 0)
    const = lambda i: (0, 0)
    return pl.pallas_call(
        functools.partial(_oproj_kernel, tm=tm),
        out_shape=(
            jax.ShapeDtypeStruct((T_ALL, D_MODEL), F32),
            jax.ShapeDtypeStruct((T_ALL, D_MODEL), BF16),
            jax.ShapeDtypeStruct((T_ALL, LANES), F32),
        ),
        grid=(T_ALL // tm,),
        in_specs=[
            pl.BlockSpec((tm, D_MODEL), lambda i: (jnp.minimum(i, npt - 1), 0)),
            pl.BlockSpec((tm, D_MODEL), lambda i: (jnp.maximum(i - npt, 0), 0)),
            pl.BlockSpec((tm, D_MODEL), row),
            pl.BlockSpec((1, 1, 6 * D_MODEL), lambda i: (_mod_row(i, tm, layer), 0, 0)),
            pl.BlockSpec((D_MODEL, D_MODEL), const),
            pl.BlockSpec((1, D_MODEL), const),
            pl.BlockSpec((D_MODEL, 2 * LANES), const),
            pl.BlockSpec((1, LANES), const),
        ],
        out_specs=(
            pl.BlockSpec((tm, D_MODEL), row),
            pl.BlockSpec((tm, D_MODEL), row),
            pl.BlockSpec((tm, LANES), row),
        ),
        compiler_params=_cparams("arbitrary"),
        name="oproj_router",
    )(o_p, o_s, x, mod, wo, g_ffn, wr, br)


def _moe_kernel(h_ref, x_ref, mod_ref, comb_ref, w1_ref, w3_ref, w2_ref, o_ref):
    gt2 = _mod_slices(mod_ref[0])[5]
    h = h_ref[...]
    comb = comb_ref[...]
    acc = jnp.zeros(o_ref.shape, F32)
    for g in range(MOE_GROUPS):
        parts = []
        for e in range(EXPERTS_PER_GROUP):
            ex = EXPERTS_PER_GROUP * g + e
            a = _dot(h, w1_ref[ex])
            u = _dot(h, w3_ref[ex])
            parts.append((_silu(a) * u * comb[:, ex:ex + 1]).astype(BF16))
        hid = jnp.concatenate(parts, axis=-1)
        acc = acc + _dot(hid, w2_ref[g])
    o_ref[...] = x_ref[...] + gt2 * acc


def _moe(h, x, mod, comb, w1, w3, w2, layer, row0=0, n_rows=T_ALL):
    tm = TM_MOE
    t0 = row0 // tm
    row = lambda i: (t0 + i, 0)
    whole = lambda shape: pl.BlockSpec(shape, lambda i: (0, 0, 0), pipeline_mode=pl.Buffered(1))
    return pl.pallas_call(
        _moe_kernel,
        out_shape=jax.ShapeDtypeStruct((n_rows, D_MODEL), F32),
        grid=(n_rows // tm,),
        in_specs=[
            pl.BlockSpec((tm, D_MODEL), row),
            pl.BlockSpec((tm, D_MODEL), row),
            pl.BlockSpec((1, 1, 6 * D_MODEL), lambda i: (_mod_row(t0 + i, tm, layer), 0, 0)),
            pl.BlockSpec((tm, LANES), row),
            whole(w1.shape), whole(w3.shape), whole(w2.shape),
        ],
        out_specs=pl.BlockSpec((tm, D_MODEL), lambda i: (i, 0)),
        compiler_params=_cparams("arbitrary"),
        name="moe",
    )(h, x, mod, comb, w1, w3, w2)


def _rope_tables(d_rot, lane0, g):
    n = DEC_SEQ
    half = d_rot // 2
    n_freq = d_rot // 4
    t = jnp.arange(n)
    rowp = (t // GRID_W).astype(F32)
    colp = (t % GRID_W).astype(F32)
    inv = jnp.power(ROPE_BASE, -jnp.arange(n_freq, dtype=F32) / n_freq)
    ang = jnp.concatenate([rowp[:, None] * inv, colp[:, None] * inv], axis=-1)
    cos, sin = jnp.cos(ang), jnp.sin(ang)
    c = jnp.ones((n, LANES), F32).at[:, lane0:lane0 + d_rot].set(jnp.concatenate([cos, cos], -1))
    sa = jnp.zeros((n, LANES), F32).at[:, lane0:lane0 + half].set(-sin)
    sb = jnp.zeros((n, LANES), F32).at[:, lane0 + half:lane0 + d_rot].set(sin)
    ident = (jnp.ones((T_PROMPT, LANES), F32), jnp.zeros((T_PROMPT, LANES), F32),
             jnp.zeros((T_PROMPT, LANES), F32))
    c, sa, sb = (jnp.concatenate([i_, t_], axis=0) for i_, t_ in zip(ident, (c, sa, sb)))
    return c * g, sa * jnp.roll(g, -half, axis=1), sb * jnp.roll(g, half, axis=1)


def _transposed(tabs):
    return tuple(t.T for t in tabs)


def _pad_heads(w, n_heads, d):
    k = w.shape[0]
    w = w.reshape(k, n_heads, d)
    return jnp.pad(w, ((0, 0), (0, 0), (0, LANES - d))).reshape(k, n_heads * LANES)


def _pad_vec(g, scale=1.0):
    return jnp.pad(g.astype(F32) * scale, (0, LANES - g.shape[0])).reshape(1, LANES)


def _pad_rows_to_heads(x, n_heads, d):
    r = x.shape[0]
    x = x.reshape(r, n_heads, d)
    return jnp.pad(x, ((0, 0), (0, 0), (0, LANES - d))).reshape(r, n_heads * LANES)


def kernel(x_prompt, x_sample, cache_mla_ckv, cache_mla_kpe, cache_gqa_k, cache_gqa_v, cache_diff_k, cache_diff_v, c, c_ctx, g_mix, g_ffn, w_mod, b_mod, w_router, b_router, w_e1, w_e3, w_e2, mla_w_dq, mla_g_q, mla_w_uq, mla_w_dkv, mla_g_kv, mla_w_ukv, mla_g_qn, mla_g_kn, mla_w_o, gqa_w_qkv, gqa_g_qn, gqa_g_kn, gqa_w_o, diff_w_qkv, diff_g_qn, diff_g_kn, diff_lam_q1, diff_lam_k1, diff_lam_q2, diff_lam_k2, diff_g_sub, diff_w_o):
    d = D_MODEL
    x = jnp.concatenate([x_prompt.reshape(T_PROMPT, d), x_sample.reshape(T_SAMPLE, d)], axis=0)

    c_all = jnp.concatenate([c_ctx[None, :], c, jnp.zeros((MOD_ROWS - 1 - DEC_BATCH, d), F32)], axis=0)
    mod = _modulation(c_all, w_mod, b_mod).reshape(DEPTH * MOD_ROWS, 1, 6 * d)

    w_hi = w_router.astype(BF16)
    w_lo = (w_router - w_hi.astype(F32)).astype(BF16)
    pad_r = ((0, 0), (0, LANES - MOE_EXPERTS))
    wr = jnp.concatenate([jnp.pad(w_hi, pad_r), jnp.pad(w_lo, pad_r)], axis=1)
    br = _pad_vec(b_router)

    w1b, w3b, w2b = w_e1.astype(BF16), w_e3.astype(BF16), w_e2.astype(BF16)

    def tables(d_rot, lane0, gq, gk):
        return (_transposed(_rope_tables(d_rot, lane0, gq)), _rope_tables(d_rot, lane0, gk))

    new_mla, new_gqa, new_diff = [], [], []
    for i in range(DEPTH):
        kind, j = i % 3, i // 3
        g_mix_i = g_mix[i].reshape(1, d)
        if kind == 0:
            w_down = jnp.concatenate(
                [mla_w_dq[j], mla_w_dkv[j],
                 jnp.zeros((d, LANES - MLA_ROPE), F32)], axis=1).astype(BF16)
            wuq = _pad_heads(mla_w_uq[j], MLA_HEADS, MLA_QK).astype(BF16)
            ukv = mla_w_ukv[j].reshape(MLA_KV_RANK, MLA_HEADS, MLA_NOPE + MLA_V)
            wk_nope = jnp.pad(ukv[:, :, :MLA_NOPE], ((0, 0), (0, 0), (0, LANES - MLA_NOPE)))
            place = jnp.zeros((LANES, MLA_HEADS, LANES), F32)
            r = jnp.arange(MLA_ROPE)
            place = place.at[r, :, MLA_NOPE + r].set(1.0)
            wk = jnp.concatenate([wk_nope.reshape(MLA_KV_RANK, -1), place.reshape(LANES, -1)],
                                 axis=0).astype(BF16)
            wv = ukv[:, :, MLA_NOPE:].reshape(MLA_KV_RANK, MLA_HEADS * MLA_V).astype(BF16)
            gqn = _pad_vec(mla_g_qn[j], LOG2E * MLA_QK ** -0.5)
            gkn = _pad_vec(mla_g_kn[j])
            y1 = _mla_down(x, mod, g_mix_i, w_down, i)
            q, k, v, ckv_c, kpe_c = _mla_up(
                y1, mla_g_q[j].reshape(1, -1), wuq, mla_g_kv[j].reshape(1, -1), wk, wv, gkn,
                *tables(MLA_ROPE, MLA_NOPE, gqn, gkn))
            cache_kpe = jnp.pad(cache_mla_kpe[:, j].reshape(DEC_BATCH * PAST_LEN, MLA_ROPE),
                                ((0, 0), (0, LANES - MLA_ROPE)))
            kc, vc = _mla_cache_expand(cache_mla_ckv[:, j].reshape(DEC_BATCH * PAST_LEN, MLA_KV_RANK),
                                       cache_kpe, wk, wv, gkn)
            o_p, o_s = _attention(q, k, v, kc, vc, (), mode="pair", k_shared=False, lam_init=0.0)
            w_o = mla_w_o[j].astype(BF16)
            new_mla.append((ckv_c[:T_PROMPT].reshape(BATCH, SEQ, MLA_KV_RANK),
                            kpe_c[:T_PROMPT, :MLA_ROPE].reshape(BATCH, SEQ, MLA_ROPE)))
        elif kind == 1:
            nq, nk = GQA_Q_HEADS, GQA_KV_HEADS
            wq_, wk_, wv_ = jnp.split(gqa_w_qkv[j], [nq * GQA_HD, (nq + nk) * GQA_HD], axis=1)
            wv_ = wv_.reshape(d, nk, GQA_HD)
            w = jnp.concatenate([_pad_heads(wq_, nq, GQA_HD), _pad_heads(wk_, nk, GQA_HD),
                                 jnp.concatenate([wv_, wv_], axis=-1).reshape(d, nk * LANES)],
                                axis=1).astype(BF16)
            gq, gk = _pad_vec(gqa_g_qn[j], LOG2E * GQA_HD ** -0.5), _pad_vec(gqa_g_kn[j])
            q, k, v, k_c, v_c = _qkv_proj(
                x, mod, g_mix_i, w, gk, *tables(GQA_HD, 0, gq, gk), i,
                nq=nq, nk=nk, nv=nk * LANES, d_real=GQA_HD, half=GQA_HD // 2)
            rows = DEC_BATCH * PAST_LEN
            kc = _pad_rows_to_heads(cache_gqa_k[:, j].reshape(rows, nk * GQA_HD), nk, GQA_HD).astype(BF16)
            cv = cache_gqa_v[:, j].reshape(rows, nk, GQA_HD)
            vc = jnp.concatenate([cv, cv], axis=-1).reshape(rows, nk * LANES).astype(BF16).T
            o_p, o_s = _attention(q, k, v, kc, vc, (), mode="pair", k_shared=True, lam_init=0.0)
            w_o = gqa_w_o[j].astype(BF16)
            new_gqa.append((k_c[:T_PROMPT].reshape(BATCH, SEQ, nk, LANES)[..., :GQA_HD],
                            v_c[:T_PROMPT].reshape(BATCH, SEQ, nk, LANES)[..., :GQA_HD]))
        else:
            nh = DIFF_HEADS
            lam_init = 0.8 - 0.6 * math.exp(-0.3 * i)
            wq_, wk_, wv_ = jnp.split(diff_w_qkv[j], 3, axis=1)
            w = jnp.concatenate([_pad_heads(wq_, 2 * nh, DIFF_HD), _pad_heads(wk_, 2 * nh, DIFF_HD),
                                 wv_], axis=1).astype(BF16)
            gq, gk = _pad_vec(diff_g_qn[j], LOG2E * DIFF_HD ** -0.5), _pad_vec(diff_g_kn[j])
            q, k, v, k_c, v_c = _qkv_proj(
                x, mod, g_mix_i, w, gk, *tables(DIFF_HD, 0, gq, gk), i,
                nq=2 * nh, nk=2 * nh, nv=nh * 2 * DIFF_HD, d_real=DIFF_HD, half=DIFF_HD // 2)
            rows = DEC_BATCH * PAST_LEN
            kc = _pad_rows_to_heads(cache_diff_k[:, j].reshape(rows, 2 * nh * DIFF_HD), 2 * nh,
                                    DIFF_HD).astype(BF16)
            vc = cache_diff_v[:, j].reshape(rows, nh * 2 * DIFF_HD).astype(BF16).T
            lam_p = jnp.concatenate([_pad_vec(diff_lam_q1[j]), _pad_vec(diff_lam_k1[j]),
                                     _pad_vec(diff_lam_q2[j]), _pad_vec(diff_lam_k2[j])], axis=0)
            o_p, o_s = _attention(q, k, v, kc, vc, (lam_p, diff_g_sub[j].reshape(1, LANES)),
                                  mode="diff", k_shared=False, lam_init=lam_init)
            w_o = diff_w_o[j].astype(BF16)
            new_diff.append((k_c[:T_PROMPT].reshape(BATCH, SEQ, nh, 2, LANES)[..., :DIFF_HD],
                             v_c[:T_PROMPT].reshape(BATCH, SEQ, nh, 2 * DIFF_HD)))

        x, h2, comb = _oproj(o_p, o_s, x, mod, w_o, g_ffn[i].reshape(1, d), wr, br, i)
        w2g = w2b[i].reshape(MOE_GROUPS, EXPERTS_PER_GROUP * MOE_DIM, d)
        if i + 1 < DEPTH:
            x = _moe(h2, x, mod, comb, w1b[i], w3b[i], w2g, i)
        else:
            y_prompt = _moe(h2, x, mod, comb, w1b[i], w3b[i], w2g, i, 0, T_PROMPT)
            y_sample = _moe(h2, x, mod, comb, w1b[i], w3b[i], w2g, i, T_PROMPT, T_SAMPLE)

    y_prompt = y_prompt.reshape(BATCH, SEQ, d)
    y_sample = y_sample.reshape(DEC_BATCH, DEC_SEQ, d)
    stack = lambda items, k: jnp.stack([t[k] for t in items], axis=1)
    return (y_prompt, y_sample, stack(new_mla, 0), stack(new_mla, 1), stack(new_gqa, 0),
            stack(new_gqa, 1), stack(new_diff, 0), stack(new_diff, 1))
```

```python
import functools
import math

import jax
import jax.numpy as jnp
from jax import lax
from jax.experimental import pallas as pl
from jax.experimental.pallas import tpu as pltpu

F32 = jnp.float32
BF16 = jnp.bfloat16

D_MODEL = 1024
BATCH, SEQ = 16, 256
DEC_BATCH, DEC_SEQ = 2, 4096
PAST_LEN = 512
DEPTH = 4
GRID_W = 64
EPS = 1e-6
ROPE_BASE = 10000.0
LOG2E = math.log2(math.e)
MLA_HEADS, MLA_NOPE, MLA_ROPE, MLA_QK, MLA_V = 16, 64, 32, 96, 64
MLA_Q_RANK, MLA_KV_RANK = 384, 256
GQA_Q_HEADS, GQA_KV_HEADS, GQA_HD = 16, 4, 64
DIFF_HEADS, DIFF_HD = 8, 64
MOE_EXPERTS, MOE_GROUPS, EXPERTS_PER_GROUP, MOE_DIM = 16, 4, 4, 256

T_PROMPT = BATCH * SEQ
T_SAMPLE = DEC_BATCH * DEC_SEQ
T_ALL = T_PROMPT + T_SAMPLE

LANES = 128
VMEM_LIMIT_BYTES = 56 * 1024 * 1024

TM_PROJ = 256
TM_MOE = 512
MOE_SUB = 160
TQ_SAMPLE = 1024
PROMPT_LOOKAHEAD = 4
Q_SPLIT = 4
CK = 512
MOD_NT = 1536
MOD_ROWS = 8


def _cparams(*sem):
    return pltpu.CompilerParams(dimension_semantics=sem, vmem_limit_bytes=VMEM_LIMIT_BYTES)


def _dot(a, b):
    return jnp.dot(a, b, preferred_element_type=F32)


def _rms(x, g):
    ms = jnp.mean(x * x, axis=-1, keepdims=True)
    return x * lax.rsqrt(ms + EPS) * g


def _norm_mod(x, g, scale, shift):
    return _rms(x, g) * (1.0 + scale) + shift


def _silu(x):
    return x / (1.0 + jnp.exp(-x))


def _mod_slices(m):
    d = D_MODEL
    return [m[:, k * d:(k + 1) * d] for k in range(6)]


def _mod_kernel(c_ref, w_ref, b_ref, o_ref):
    s = _silu(c_ref[...]).astype(BF16)
    o_ref[0] = _dot(s, w_ref[0].astype(BF16)) + b_ref[0]


def _modulation(c_all, w_mod, b_mod):
    n = 6 * D_MODEL
    return pl.pallas_call(
        _mod_kernel,
        out_shape=jax.ShapeDtypeStruct((DEPTH, MOD_ROWS, n), F32),
        grid=(DEPTH, n // MOD_NT),
        in_specs=[
            pl.BlockSpec((MOD_ROWS, D_MODEL), lambda l, j: (0, 0)),
            pl.BlockSpec((1, D_MODEL, MOD_NT), lambda l, j: (l, 0, j)),
            pl.BlockSpec((1, 1, MOD_NT), lambda l, j: (l, 0, j)),
        ],
        out_specs=pl.BlockSpec((1, MOD_ROWS, MOD_NT), lambda l, j: (l, 0, j)),
        compiler_params=_cparams("arbitrary", "arbitrary"),
        name="modulation",
    )(c_all, w_mod, b_mod.reshape(DEPTH, 1, n))


def _mod_row(i, tm, layer):
    r = jnp.where(i * tm < T_PROMPT, 0, 1 + (i * tm - T_PROMPT) // DEC_SEQ)
    return layer * MOD_ROWS + r


def _table_blk(i, tm):
    npt = T_PROMPT // tm
    return jnp.where(i < npt, i, npt + (i - npt) % (DEC_SEQ // tm))


def _row_specs(x, tm):
    if not isinstance(x, tuple):
        return [pl.BlockSpec((tm, D_MODEL), lambda i: (i, 0))]
    npt = T_PROMPT // tm
    return [pl.BlockSpec((tm, D_MODEL), lambda i: (jnp.minimum(i, npt - 1), 0)),
            pl.BlockSpec((tm, D_MODEL), lambda i: (jnp.maximum(i - npt, 0), 0))]


def _read_rows(refs, tm):
    if len(refs) == 1:
        return refs[0][...]
    return jnp.where(pl.program_id(0) < T_PROMPT // tm, refs[0][...], refs[1][...])


def _cache_blk(i, tm):
    return jnp.minimum(i, T_PROMPT // tm)


def _q_epilogue(y, g_t, tabs, d_real, half):
    ct, sat, sbt = tabs
    yt = y.T
    ms = jnp.sum(yt * yt, axis=0, keepdims=True) * (1.0 / d_real)
    qn = yt * lax.rsqrt(ms + EPS) * g_t
    up = jnp.concatenate([qn[half:], qn[:half]], axis=0)
    dn = jnp.concatenate([qn[LANES - half:], qn[:LANES - half]], axis=0)
    return qn * ct + up * sat + dn * sbt


def _k_epilogue(y, g, tabs, d_real, half):
    ms = jnp.sum(y * y, axis=-1, keepdims=True) * (1.0 / d_real)
    kc = y * lax.rsqrt(ms + EPS) * g
    if tabs is None:
        return kc, kc
    c, sa, sb = tabs
    return kc, kc * c + pltpu.roll(kc, LANES - half, 1) * sa + pltpu.roll(kc, half, 1) * sb


def _run_segments(segments, epilogue):
    prev = None
    for idx, (lhs, w_ref, col) in enumerate(segments):
        y = _dot(lhs, w_ref[:, col:col + 2 * LANES])
        if prev is not None:
            epilogue(*prev)
        prev = (idx, y)
    epilogue(*prev)


def _store_head_blocks(idx, y, nq, nk, qtabs, ktabs, gq_t, gk, d_real, half,
                       qt_ref, k_ref, vt_ref, kc_ref, vc_ref):
    for hb in range(2):
        j = 2 * idx + hb
        blk = y[:, hb * LANES:(hb + 1) * LANES]
        if j < nq:
            qt_ref[j * LANES:(j + 1) * LANES, :] = _q_epilogue(blk, gq_t, qtabs, d_real, half).astype(BF16)
        elif j < nq + nk:
            sl = slice((j - nq) * LANES, (j - nq + 1) * LANES)
            kc, kr = _k_epilogue(blk, gk, ktabs, d_real, half)
            if kc_ref is not None:
                kc_ref[:, sl] = kc
            k_ref[:, sl] = kr.astype(BF16)
        else:
            sl = slice((j - nq - nk) * LANES, (j - nq - nk + 1) * LANES)
            if vc_ref is not None:
                vc_ref[:, sl] = blk
            vt_ref[sl, :] = blk.T.astype(BF16)


def _qkv_kernel(x_ref, mod_ref, g_ref, w_ref, gq_ref, gk_ref, qc_ref, qsa_ref, qsb_ref,
                kc_t_ref, ksa_ref, ksb_ref, qt_ref, k_ref, vt_ref, kc_ref, vc_ref,
                *, nq, nk, d_real, half):
    sh, sc = _mod_slices(mod_ref[0])[:2]
    h = _norm_mod(x_ref[...], g_ref[...], sc, sh).astype(BF16)
    qtabs = (qc_ref[...], qsa_ref[...], qsb_ref[...])
    ktabs = (kc_t_ref[...], ksa_ref[...], ksb_ref[...])
    gk = gk_ref[...]
    epilogue = functools.partial(
        _store_head_blocks, nq=nq, nk=nk, qtabs=qtabs, ktabs=ktabs, gq_t=gq_ref[...], gk=gk,
        d_real=d_real,
        half=half, qt_ref=qt_ref, k_ref=k_ref, vt_ref=vt_ref, kc_ref=kc_ref, vc_ref=vc_ref)
    n_seg = w_ref.shape[1] // (2 * LANES)
    _run_segments([(h, w_ref, s * 2 * LANES) for s in range(n_seg)], epilogue)


def _table_specs(tm):
    qtab = pl.BlockSpec((LANES, tm), lambda i: (0, _table_blk(i, tm)))
    ktab = pl.BlockSpec((tm, LANES), lambda i: (_table_blk(i, tm), 0))
    return [qtab, qtab, qtab, ktab, ktab, ktab]


def _qkv_proj(x, mod, g, w, gq_t, gk, qtabs, ktabs, layer, *, nq, nk, nv, d_real, half):
    tm = TM_PROJ
    n = w.shape[1]
    wq, wk = nq * LANES, nk * LANES
    row = lambda i: (i, 0)
    col = lambda i: (0, i)
    const = lambda i: (0, 0)
    cache = lambda width: pl.BlockSpec((tm, width), lambda i: (_cache_blk(i, tm), 0))
    return pl.pallas_call(
        functools.partial(_qkv_kernel, nq=nq, nk=nk, d_real=d_real, half=half),
        out_shape=(
            jax.ShapeDtypeStruct((wq, T_ALL), BF16),
            jax.ShapeDtypeStruct((T_ALL, wk), BF16),
            jax.ShapeDtypeStruct((nv, T_ALL), BF16),
            jax.ShapeDtypeStruct((T_PROMPT + tm, wk), F32),
            jax.ShapeDtypeStruct((T_PROMPT + tm, nv), F32),
        ),
        grid=(T_ALL // tm,),
        in_specs=[
            pl.BlockSpec((tm, D_MODEL), row),
            pl.BlockSpec((1, 1, 6 * D_MODEL), lambda i: (_mod_row(i, tm, layer), 0, 0)),
            pl.BlockSpec((1, D_MODEL), const),
            pl.BlockSpec((D_MODEL, n), const),
            pl.BlockSpec((LANES, tm), const),
            pl.BlockSpec((1, LANES), const),
        ] + _table_specs(tm),
        out_specs=(
            pl.BlockSpec((wq, tm), col),
            pl.BlockSpec((tm, wk), row),
            pl.BlockSpec((nv, tm), col),
            cache(wk),
            cache(nv),
        ),
        compiler_params=_cparams("arbitrary"),
        name="qkv_proj",
    )(x, mod, g, w, gq_t, gk, *qtabs, *ktabs)


def _down_kernel(*refs, tm):
    mod_ref, g_ref, w_ref, y_ref = refs[-4:]
    sh, sc = _mod_slices(mod_ref[0])[:2]
    h = _norm_mod(_read_rows(refs[:-4], tm), g_ref[...], sc, sh).astype(BF16)
    y_ref[...] = _dot(h, w_ref[...])


def _mla_down(x, mod, g, w, layer):
    tm = TM_PROJ
    n = w.shape[1]
    xs = x if isinstance(x, tuple) else (x,)
    return pl.pallas_call(
        functools.partial(_down_kernel, tm=tm),
        out_shape=jax.ShapeDtypeStruct((T_ALL, n), F32),
        grid=(T_ALL // tm,),
        in_specs=_row_specs(x, tm) + [
            pl.BlockSpec((1, 1, 6 * D_MODEL), lambda i: (_mod_row(i, tm, layer), 0, 0)),
            pl.BlockSpec((1, D_MODEL), lambda i: (0, 0)),
            pl.BlockSpec((D_MODEL, n), lambda i: (0, 0)),
        ],
        out_specs=pl.BlockSpec((tm, n), lambda i: (i, 0)),
        compiler_params=_cparams("arbitrary"),
        name="mla_down",
    )(*xs, mod, g, w)


MLA_NQ = MLA_HEADS
MLA_NV = MLA_HEADS * MLA_V // LANES


def _mla_up_kernel(y_ref, gq_ref, wuq_ref, gkv_ref, wk_ref, wv_ref, gqn_ref, gkn_ref,
                   qc_ref, qsa_ref, qsb_ref, kc_t_ref, ksa_ref, ksb_ref,
                   qt_ref, k_ref, vt_ref, ckv_ref, kpe_ref):
    y = y_ref[...]
    cqn = _rms(y[:, :MLA_Q_RANK], gq_ref[...]).astype(BF16)
    ckvn = _rms(y[:, MLA_Q_RANK:MLA_Q_RANK + MLA_KV_RANK], gkv_ref[...])
    kpe_blk = y[:, MLA_Q_RANK + MLA_KV_RANK:]
    ckv_ref[...] = ckvn
    kpe_ref[...] = kpe_blk
    ckv_b = ckvn.astype(BF16)
    kcat = jnp.concatenate([ckv_b, kpe_blk.astype(BF16)], axis=-1)
    qtabs = (qc_ref[...], qsa_ref[...], qsb_ref[...])
    ktabs = (kc_t_ref[...], ksa_ref[...], ksb_ref[...])
    epilogue = functools.partial(
        _store_head_blocks, nq=MLA_NQ, nk=MLA_HEADS, qtabs=qtabs, ktabs=ktabs,
        gq_t=gqn_ref[...], gk=gkn_ref[...],
        d_real=MLA_QK, half=MLA_ROPE // 2, qt_ref=qt_ref, k_ref=k_ref, vt_ref=vt_ref,
        kc_ref=None, vc_ref=None)
    seg = 2 * LANES
    segments = ([(cqn, wuq_ref, s * seg) for s in range(MLA_NQ // 2)]
                + [(kcat, wk_ref, s * seg) for s in range(MLA_HEADS // 2)]
                + [(ckv_b, wv_ref, s * seg) for s in range(MLA_NV // 2)])
    _run_segments(segments, epilogue)


def _mla_up(y1, gq, wuq, gkv, wk, wv, gqn_t, gkn, qtabs, ktabs):
    tm = TM_PROJ
    hq = MLA_HEADS * LANES
    hv = MLA_HEADS * MLA_V
    row = lambda i: (i, 0)
    col = lambda i: (0, i)
    const = lambda i: (0, 0)
    cache = lambda width: pl.BlockSpec((tm, width), lambda i: (_cache_blk(i, tm), 0))
    return pl.pallas_call(
        _mla_up_kernel,
        out_shape=(
            jax.ShapeDtypeStruct((hq, T_ALL), BF16),
            jax.ShapeDtypeStruct((T_ALL, hq), BF16),
            jax.ShapeDtypeStruct((hv, T_ALL), BF16),
            jax.ShapeDtypeStruct((T_PROMPT + tm, MLA_KV_RANK), F32),
            jax.ShapeDtypeStruct((T_PROMPT + tm, LANES), F32),
        ),
        grid=(T_ALL // tm,),
        in_specs=[
            pl.BlockSpec((tm, y1.shape[1]), row),
            pl.BlockSpec((1, MLA_Q_RANK), const),
            pl.BlockSpec(wuq.shape, const),
            pl.BlockSpec((1, MLA_KV_RANK), const),
            pl.BlockSpec(wk.shape, const),
            pl.BlockSpec(wv.shape, const),
            pl.BlockSpec((LANES, tm), const),
            pl.BlockSpec((1, LANES), const),
        ] + _table_specs(tm),
        out_specs=(
            pl.BlockSpec((hq, tm), col),
            pl.BlockSpec((tm, hq), row),
            pl.BlockSpec((hv, tm), col),
            cache(MLA_KV_RANK),
            cache(LANES),
        ),
        compiler_params=_cparams("arbitrary"),
        name="mla_up",
    )(y1, gq, wuq, gkv, wk, wv, gqn_t, gkn, *qtabs, *ktabs)


def _mla_cache_kernel(ckv_ref, kpe_ref, wk_ref, wv_ref, gkn_ref, k_ref, vt_ref):
    ckv_b = ckv_ref[...].astype(BF16)
    kcat = jnp.concatenate([ckv_b, kpe_ref[...].astype(BF16)], axis=-1)
    epilogue = functools.partial(
        _store_head_blocks, nq=0, nk=MLA_HEADS, qtabs=None, ktabs=None, gq_t=None,
        gk=gkn_ref[...],
        d_real=MLA_QK, half=MLA_ROPE // 2, qt_ref=None, k_ref=k_ref, vt_ref=vt_ref,
        kc_ref=None, vc_ref=None)
    seg = 2 * LANES
    segments = ([(kcat, wk_ref, s * seg) for s in range(MLA_HEADS // 2)]
                + [(ckv_b, wv_ref, s * seg) for s in range(MLA_NV // 2)])
    _run_segments(segments, epilogue)


def _mla_cache_expand(ckv, kpe_blk, wk, wv, gkn):
    rows = ckv.shape[0]
    tm = TM_PROJ
    hq = MLA_HEADS * LANES
    hv = MLA_HEADS * MLA_V
    row = lambda i: (i, 0)
    const = lambda i: (0, 0)
    return pl.pallas_call(
        _mla_cache_kernel,
        out_shape=(jax.ShapeDtypeStruct((rows, hq), BF16), jax.ShapeDtypeStruct((hv, rows), BF16)),
        grid=(rows // tm,),
        in_specs=[
            pl.BlockSpec((tm, MLA_KV_RANK), row),
            pl.BlockSpec((tm, LANES), row),
            pl.BlockSpec(wk.shape, const),
            pl.BlockSpec(wv.shape, const),
            pl.BlockSpec((1, LANES), const),
        ],
        out_specs=(pl.BlockSpec((tm, hq), row), pl.BlockSpec((hv, tm), lambda i: (0, i))),
        compiler_params=_cparams("arbitrary"),
        name="mla_cache_expand",
    )(ckv, kpe_blk, wk, wv, gkn)


def _attend(s, vt, state):
    m_cur = jnp.max(s, axis=0, keepdims=True)
    if state is None:
        p = jnp.exp2(s - m_cur)
        return m_cur, jnp.sum(p, axis=0, keepdims=True), _dot(vt, p.astype(BF16))
    m, l, acc = state
    m_new = jnp.maximum(m, m_cur)
    alpha = jnp.exp2(m - m_new)
    p = jnp.exp2(s - m_new)
    l = alpha * l + jnp.sum(p, axis=0, keepdims=True)
    return m_new, l, alpha * acc + _dot(vt, p.astype(BF16))


def _attn_kernel(*refs, mode, n_blocks, k_shared, has_cache, n_keys, ck, lam_init):
    refs = list(refs)
    qt_ref, k_ref, vt_ref = refs[:3]
    pos = 3
    if has_cache:
        kc_ref, vct_ref = refs[pos:pos + 2]
        pos += 2
    if mode == "diff":
        lam_ref, gsub_ref = refs[pos:pos + 2]
        pos += 2
        lp = lam_ref[...]
        lam = (jnp.exp(jnp.sum(lp[0:1] * lp[1:2], axis=-1, keepdims=True))
               - jnp.exp(jnp.sum(lp[2:3] * lp[3:4], axis=-1, keepdims=True)) + lam_init)
    o_ref = refs[pos]

    chunks = []
    if has_cache:
        chunks += [(kc_ref, vct_ref, 0, PAST_LEN)]
    chunks += [(k_ref, vt_ref, c * ck, ck) for c in range(n_keys // ck)]

    n_sub = Q_SPLIT if has_cache else 1
    w = qt_ref.shape[1] // n_sub
    streams = []
    for jb in range(n_blocks):
        if k_shared:
            k_sls = [slice((jb // 2) * LANES, (jb // 2 + 1) * LANES)] * 2
            v_sl = k_sls[0]
        else:
            k_sls = [slice((2 * jb + t) * LANES, (2 * jb + t + 1) * LANES) for t in range(2)]
            v_sl = slice(jb * LANES, (jb + 1) * LANES)
        for sub in range(n_sub):
            for t in range(2):
                q_sl = (slice((2 * jb + t) * LANES, (2 * jb + t + 1) * LANES),
                        slice(sub * w, (sub + 1) * w))
                streams.append((q_sl, k_sls[t], v_sl))

    def scores(item):
        x, c = item
        q_sl, k_sl, _ = streams[x]
        kr, _, start, n_ = chunks[c]
        return _dot(kr[start:start + n_, k_sl], qt_ref[q_sl])

    items = [(x, c) for c in range(len(chunks)) for x in range(len(streams))]
    ahead = len(streams) if has_cache else PROMPT_LOOKAHEAD
    sc = {k: scores(items[k]) for k in range(min(ahead, len(items)))}
    st = [None] * len(streams)
    for k, (x, c) in enumerate(items):
        _, vr, start, n_ = chunks[c]
        st[x] = _attend(sc.pop(k), vr[streams[x][2], start:start + n_], st[x])
        if k + ahead < len(items):
            sc[k + ahead] = scores(items[k + ahead])

    for jb in range(n_blocks):
        base = jb * n_sub * 2
        cat = lambda k_, t: jnp.concatenate(
            [st[base + 2 * sub + t][k_] for sub in range(n_sub)], axis=1)
        oa = cat(2, 0) * (1.0 / cat(1, 0))
        ob = cat(2, 1) * (1.0 / cat(1, 1))
        if mode == "pair":
            row = lax.broadcasted_iota(jnp.int32, oa.shape, 0)
            o = jnp.where(row < LANES // 2, oa, ob).T
        else:
            o = oa - lam * ob
            ms = jnp.mean(o * o, axis=0, keepdims=True)
            o = (o * lax.rsqrt(ms + EPS)).T * (gsub_ref[...] * (1.0 - lam_init))
        o_ref[:, jb * LANES:(jb + 1) * LANES] = o.astype(BF16)


def _attention(qt, k, vt, kc, vct, extra, *, mode, k_shared, lam_init):
    n_blk = D_MODEL // LANES
    extra_specs_1 = [pl.BlockSpec(e.shape, lambda b: (0, 0)) for e in extra]
    extra_specs_3 = [pl.BlockSpec(e.shape, lambda b, j, i: (0, 0)) for e in extra]

    o_p = pl.pallas_call(
        functools.partial(_attn_kernel, mode=mode, n_blocks=n_blk, k_shared=k_shared,
                          has_cache=False, n_keys=SEQ, ck=SEQ, lam_init=lam_init),
        out_shape=jax.ShapeDtypeStruct((T_PROMPT, D_MODEL), BF16),
        grid=(BATCH,),
        in_specs=[
            pl.BlockSpec((qt.shape[0], SEQ), lambda b: (0, b)),
            pl.BlockSpec((SEQ, k.shape[1]), lambda b: (b, 0)),
            pl.BlockSpec((vt.shape[0], SEQ), lambda b: (0, b)),
        ] + extra_specs_1,
        out_specs=pl.BlockSpec((SEQ, D_MODEL), lambda b: (b, 0)),
        compiler_params=_cparams("arbitrary"),
        name="attn_prompt",
    )(qt, k, vt, *extra)

    tq = TQ_SAMPLE
    kv_col = (lambda j: j // 2) if k_shared else (lambda j: j)
    kw = LANES if k_shared else 2 * LANES
    lat = T_PROMPT // DEC_SEQ
    o_s = pl.pallas_call(
        functools.partial(_attn_kernel, mode=mode, n_blocks=1, k_shared=k_shared,
                          has_cache=True, n_keys=DEC_SEQ, ck=CK, lam_init=lam_init),
        out_shape=jax.ShapeDtypeStruct((T_SAMPLE, D_MODEL), BF16),
        grid=(DEC_BATCH, n_blk, DEC_SEQ // tq),
        in_specs=[
            pl.BlockSpec((2 * LANES, tq),
                         lambda b, j, i: (j, T_PROMPT // tq + b * (DEC_SEQ // tq) + i)),
            pl.BlockSpec((DEC_SEQ, kw), lambda b, j, i: (lat + b, kv_col(j))),
            pl.BlockSpec((LANES, DEC_SEQ), lambda b, j, i: (kv_col(j), lat + b)),
            pl.BlockSpec((PAST_LEN, kw), lambda b, j, i: (b, kv_col(j))),
            pl.BlockSpec((LANES, PAST_LEN), lambda b, j, i: (kv_col(j), b)),
        ] + extra_specs_3,
        out_specs=pl.BlockSpec((tq, LANES), lambda b, j, i: (b * (DEC_SEQ // tq) + i, j)),
        compiler_params=_cparams("arbitrary", "arbitrary", "arbitrary"),
        name="attn_latent",
    )(qt, k, vt, kc, vct, *extra)
    return o_p, o_s


def _route(sel_t, sc_t, tm):
    sel = [sel_t[e:e + 1, :] for e in range(MOE_EXPERTS)]
    sc = [sc_t[e:e + 1, :] for e in range(MOE_EXPERTS)]
    gscore = []
    for g in range(MOE_GROUPS):
        a, b, c, d = sel[4 * g:4 * g + 4]
        hi1, lo1 = jnp.maximum(a, b), jnp.minimum(a, b)
        hi2, lo2 = jnp.maximum(c, d), jnp.minimum(c, d)
        top1 = jnp.maximum(hi1, hi2)
        top2 = jnp.maximum(jnp.minimum(hi1, hi2), jnp.maximum(lo1, lo2))
        gscore.append(top1 + top2)
    gmax = jnp.maximum(jnp.maximum(gscore[0], gscore[1]), jnp.maximum(gscore[2], gscore[3]))
    taken = jnp.zeros_like(gmax)
    gsel = []
    for g in range(MOE_GROUPS):
        hit = jnp.where(gscore[g] == gmax, 1.0, 0.0) * (1.0 - taken)
        gsel.append(hit)
        taken = taken + hit
    vs, ss = [], []
    for e in range(EXPERTS_PER_GROUP):
        vs.append(sum(gsel[g] * sel[4 * g + e] for g in range(MOE_GROUPS)))
        ss.append(sum(gsel[g] * sc[4 * g + e] for g in range(MOE_GROUPS)))
    ws = []
    for i in range(EXPERTS_PER_GROUP):
        beaten = jnp.zeros_like(gmax)
        for j in range(EXPERTS_PER_GROUP):
            if j < i:
                beaten = beaten + jnp.where(vs[j] >= vs[i], 1.0, 0.0)
            elif j > i:
                beaten = beaten + jnp.where(vs[j] > vs[i], 1.0, 0.0)
        ws.append(jnp.where(beaten < 2.0, ss[i], 0.0))
    denom = (ws[0] + ws[1]) + (ws[2] + ws[3])
    inv = 1.0 / denom
    n_rows = 2 * MOE_EXPERTS
    row = lax.broadcasted_iota(jnp.int32, (n_rows, tm), 0)
    comb = jnp.zeros((n_rows, tm), F32)
    for g in range(MOE_GROUPS):
        comb = jnp.where(row == MOE_EXPERTS + g, jnp.broadcast_to(gsel[g], (n_rows, tm)), comb)
        for e in range(EXPERTS_PER_GROUP):
            w = gsel[g] * ws[e] * inv
            comb = jnp.where(row == 4 * g + e, jnp.broadcast_to(w, (n_rows, tm)), comb)
    return comb


def _oproj_kernel(*refs, tm):
    op_ref, os_ref = refs[:2]
    mod_ref, wo_ref, g_ref, wr_ref, br_ref, xo_ref, h_ref, comb_ref = refs[-8:]
    _, _, gt1, sh2, sc2, _ = _mod_slices(mod_ref[0])
    o = _read_rows((op_ref, os_ref), tm)
    xn = _read_rows(refs[2:-8], tm) + gt1 * _dot(o, wo_ref[...])
    xo_ref[...] = xn
    h = _norm_mod(xn, g_ref[...], sc2, sh2)
    hi = h.astype(BF16)
    lo = (h - hi.astype(F32)).astype(BF16)
    h_ref[...] = hi
    wr = wr_ref[...]
    r1 = _dot(hi, wr)
    logits = r1[:, :LANES] + r1[:, LANES:] + _dot(lo, wr[:, :LANES])
    scores = 1.0 / (1.0 + jnp.exp(-logits))
    sel = scores + br_ref[...]
    comb = _route(sel.T, scores.T, tm)
    comb = jnp.concatenate([comb, jnp.zeros((LANES - 2 * MOE_EXPERTS, tm), F32)], axis=0)
    comb_ref[...] = comb.T


def _oproj(o_p, o_s, x, mod, wo, g_ffn, wr, br, layer):
    tm = TM_PROJ
    row = lambda i: (i, 0)
    const = lambda i: (0, 0)
    xs = x if isinstance(x, tuple) else (x,)
    return pl.pallas_call(
        functools.partial(_oproj_kernel, tm=tm),
        out_shape=(
            jax.ShapeDtypeStruct((T_ALL, D_MODEL), F32),
            jax.ShapeDtypeStruct((T_ALL, D_MODEL), BF16),
            jax.ShapeDtypeStruct((T_ALL, LANES), F32),
        ),
        grid=(T_ALL // tm,),
        in_specs=_row_specs((o_p, o_s), tm) + _row_specs(x, tm) + [
            pl.BlockSpec((1, 1, 6 * D_MODEL), lambda i: (_mod_row(i, tm, layer), 0, 0)),
            pl.BlockSpec((D_MODEL, D_MODEL), const),
            pl.BlockSpec((1, D_MODEL), const),
            pl.BlockSpec((D_MODEL, 2 * LANES), const),
            pl.BlockSpec((1, LANES), const),
        ],
        out_specs=(
            pl.BlockSpec((tm, D_MODEL), row),
            pl.BlockSpec((tm, D_MODEL), row),
            pl.BlockSpec((tm, LANES), row),
        ),
        compiler_params=_cparams("arbitrary"),
        name="oproj_router",
    )(o_p, o_s, *xs, mod, wo, g_ffn, wr, br)


def _moe_kernel(h_ref, x_ref, mod_ref, comb_ref, tri_ref, w1_ref, w3_ref, w2_ref, o_ref, acc_ref):
    gt2 = _mod_slices(mod_ref[0])[5]
    comb = comb_ref[...]
    comb_hi = comb.astype(BF16)
    comb_lo = (comb - comb_hi.astype(F32)).astype(BF16)
    rank = _dot(tri_ref[...], comb_hi)
    rank_t, comb_t = rank.T, comb.T
    sub_r = lax.broadcasted_iota(jnp.int32, (MOE_SUB, 1), 0).astype(F32)
    sub_c = lax.broadcasted_iota(jnp.int32, (1, MOE_SUB), 1).astype(F32)
    acc_ref[...] = jnp.zeros(acc_ref.shape, F32)
    for g in range(MOE_GROUPS):
        lane = MOE_EXPERTS + g
        member_r = comb_t[lane:lane + 1, :]
        member_c = comb[:, lane:lane + 1]
        slot_r = jnp.where(member_r > 0.5, rank_t[lane:lane + 1, :], -1.0)
        slot_c = jnp.where(member_c > 0.5, rank[:, lane:lane + 1], -1.0)
        count = jnp.sum(member_r).astype(jnp.int32)
        n_blk = lax.div(count + (MOE_SUB - 1), MOE_SUB)

        def body(b, carry):
            base = (b * MOE_SUB).astype(F32)
            sel = jnp.where(slot_r - base == sub_r, 1.0, 0.0).astype(BF16)
            sel_t = jnp.where(slot_c - base == sub_c, 1.0, 0.0).astype(BF16)
            hc = _dot(sel, h_ref[...]).astype(BF16)
            wc = _dot(sel, comb_hi) + _dot(sel, comb_lo)
            parts = []
            for e in range(EXPERTS_PER_GROUP):
                ex = EXPERTS_PER_GROUP * g + e
                a = _dot(hc, w1_ref[ex])
                u = _dot(hc, w3_ref[ex])
                parts.append((_silu(a) * u * wc[:, ex:ex + 1]).astype(BF16))
            og = _dot(jnp.concatenate(parts, axis=-1), w2_ref[g]).astype(BF16)
            acc_ref[...] += _dot(sel_t, og)
            return carry

        lax.fori_loop(0, n_blk, body, 0)
    o_ref[...] = x_ref[...] + gt2 * acc_ref[...]


def _moe(h, x, mod, comb, tri, w1, w3, w2, layer, row0=0, n_rows=T_ALL):
    tm = TM_MOE
    t0 = row0 // tm
    row = lambda i: (t0 + i, 0)
    whole = lambda shape: pl.BlockSpec(shape, lambda i: (0, 0, 0), pipeline_mode=pl.Buffered(1))
    return pl.pallas_call(
        _moe_kernel,
        out_shape=jax.ShapeDtypeStruct((n_rows, D_MODEL), F32),
        grid=(n_rows // tm,),
        in_specs=[
            pl.BlockSpec((tm, D_MODEL), row),
            pl.BlockSpec((tm, D_MODEL), row),
            pl.BlockSpec((1, 1, 6 * D_MODEL), lambda i: (_mod_row(t0 + i, tm, layer), 0, 0)),
            pl.BlockSpec((tm, LANES), row),
            pl.BlockSpec((tm, tm), lambda i: (0, 0)),
            whole(w1.shape), whole(w3.shape), whole(w2.shape),
        ],
        out_specs=pl.BlockSpec((tm, D_MODEL), lambda i: (i, 0)),
        scratch_shapes=[pltpu.VMEM((tm, D_MODEL), F32)],
        compiler_params=_cparams("arbitrary"),
        name="moe",
    )(h, x, mod, comb, tri, w1, w3, w2)


def _rope_tables(d_rot, lane0):
    n = DEC_SEQ
    half = d_rot // 2
    n_freq = d_rot // 4
    t = jnp.arange(n)
    rowp = (t // GRID_W).astype(F32)
    colp = (t % GRID_W).astype(F32)
    inv = jnp.power(ROPE_BASE, -jnp.arange(n_freq, dtype=F32) / n_freq)
    ang = jnp.concatenate([rowp[:, None] * inv, colp[:, None] * inv], axis=-1)
    cos, sin = jnp.cos(ang), jnp.sin(ang)
    c = jnp.ones((n, LANES), F32).at[:, lane0:lane0 + d_rot].set(jnp.concatenate([cos, cos], -1))
    sa = jnp.zeros((n, LANES), F32).at[:, lane0:lane0 + half].set(-sin)
    sb = jnp.zeros((n, LANES), F32).at[:, lane0 + half:lane0 + d_rot].set(sin)
    ident = (jnp.ones((T_PROMPT, LANES), F32), jnp.zeros((T_PROMPT, LANES), F32),
             jnp.zeros((T_PROMPT, LANES), F32))
    tabs = tuple(jnp.concatenate([i_, t_], axis=0) for i_, t_ in zip(ident, (c, sa, sb)))
    return tuple(t_.T for t_ in tabs), tabs


def _gain_t(g):
    return jnp.broadcast_to(g.reshape(LANES, 1), (LANES, TM_PROJ))


def _pad_heads(w, n_heads, d):
    k = w.shape[0]
    w = w.reshape(k, n_heads, d)
    return jnp.pad(w, ((0, 0), (0, 0), (0, LANES - d))).reshape(k, n_heads * LANES)


def _pad_vec(g, scale=1.0):
    return jnp.pad(g.astype(F32) * scale, (0, LANES - g.shape[0])).reshape(1, LANES)


def _pad_rows_to_heads(x, n_heads, d):
    r = x.shape[0]
    x = x.reshape(r, n_heads, d)
    return jnp.pad(x, ((0, 0), (0, 0), (0, LANES - d))).reshape(r, n_heads * LANES)


def kernel(x_prompt, x_sample, cache_mla_ckv, cache_mla_kpe, cache_gqa_k, cache_gqa_v, cache_diff_k, cache_diff_v, c, c_ctx, g_mix, g_ffn, w_mod, b_mod, w_router, b_router, w_e1, w_e3, w_e2, mla_w_dq, mla_g_q, mla_w_uq, mla_w_dkv, mla_g_kv, mla_w_ukv, mla_g_qn, mla_g_kn, mla_w_o, gqa_w_qkv, gqa_g_qn, gqa_g_kn, gqa_w_o, diff_w_qkv, diff_g_qn, diff_g_kn, diff_lam_q1, diff_lam_k1, diff_lam_q2, diff_lam_k2, diff_g_sub, diff_w_o):
    d = D_MODEL
    x = (x_prompt.reshape(T_PROMPT, d), x_sample.reshape(T_SAMPLE, d))

    c_all = jnp.concatenate([c_ctx[None, :], c, jnp.zeros((MOD_ROWS - 1 - DEC_BATCH, d), F32)], axis=0)
    mod = _modulation(c_all, w_mod, b_mod).reshape(DEPTH * MOD_ROWS, 1, 6 * d)

    w_hi = w_router.astype(BF16)
    w_lo = (w_router - w_hi.astype(F32)).astype(BF16)
    pad_r = ((0, 0), (0, LANES - MOE_EXPERTS))
    wr = jnp.concatenate([jnp.pad(w_hi, pad_r), jnp.pad(w_lo, pad_r)], axis=1)
    br = _pad_vec(b_router)

    w1b, w3b, w2b = w_e1.astype(BF16), w_e3.astype(BF16), w_e2.astype(BF16)
    t_idx = jnp.arange(TM_MOE)
    tri = (t_idx[None, :] < t_idx[:, None]).astype(BF16)

    tab_mla = _rope_tables(MLA_ROPE, MLA_NOPE)
    tab_64 = _rope_tables(GQA_HD, 0)

    new_mla, new_gqa, new_diff = [], [], []
    for i in range(DEPTH):
        kind, j = i % 3, i // 3
        g_mix_i = g_mix[i].reshape(1, d)
        if kind == 0:
            w_down = jnp.concatenate(
                [mla_w_dq[j], mla_w_dkv[j],
                 jnp.zeros((d, LANES - MLA_ROPE), F32)], axis=1).astype(BF16)
            wuq = _pad_heads(mla_w_uq[j], MLA_HEADS, MLA_QK).astype(BF16)
            ukv = mla_w_ukv[j].reshape(MLA_KV_RANK, MLA_HEADS, MLA_NOPE + MLA_V)
            wk_nope = jnp.pad(ukv[:, :, :MLA_NOPE], ((0, 0), (0, 0), (0, LANES - MLA_NOPE)))
            place = jnp.zeros((LANES, MLA_HEADS, LANES), F32)
            r = jnp.arange(MLA_ROPE)
            place = place.at[r, :, MLA_NOPE + r].set(1.0)
            wk = jnp.concatenate([wk_nope.reshape(MLA_KV_RANK, -1), place.reshape(LANES, -1)],
                                 axis=0).astype(BF16)
            wv = ukv[:, :, MLA_NOPE:].reshape(MLA_KV_RANK, MLA_HEADS * MLA_V).astype(BF16)
            gqn = _pad_vec(mla_g_qn[j], LOG2E * MLA_QK ** -0.5)
            gkn = _pad_vec(mla_g_kn[j])
            y1 = _mla_down(x, mod, g_mix_i, w_down, i)
            q, k, v, ckv_c, kpe_c = _mla_up(
                y1, mla_g_q[j].reshape(1, -1), wuq, mla_g_kv[j].reshape(1, -1), wk, wv,
                _gain_t(gqn), gkn, *tab_mla)
            cache_kpe = jnp.pad(cache_mla_kpe[:, j].reshape(DEC_BATCH * PAST_LEN, MLA_ROPE),
                                ((0, 0), (0, LANES - MLA_ROPE)))
            kc, vc = _mla_cache_expand(cache_mla_ckv[:, j].reshape(DEC_BATCH * PAST_LEN, MLA_KV_RANK),
                                       cache_kpe, wk, wv, gkn)
            o_p, o_s = _attention(q, k, v, kc, vc, (), mode="pair", k_shared=False, lam_init=0.0)
            w_o = mla_w_o[j].astype(BF16)
            new_mla.append((ckv_c[:T_PROMPT].reshape(BATCH, SEQ, MLA_KV_RANK),
                            kpe_c[:T_PROMPT, :MLA_ROPE].reshape(BATCH, SEQ, MLA_ROPE)))
        elif kind == 1:
            nq, nk = GQA_Q_HEADS, GQA_KV_HEADS
            wq_, wk_, wv_ = jnp.split(gqa_w_qkv[j], [nq * GQA_HD, (nq + nk) * GQA_HD], axis=1)
            wv_ = wv_.reshape(d, nk, GQA_HD)
            w = jnp.concatenate([_pad_heads(wq_, nq, GQA_HD), _pad_heads(wk_, nk, GQA_HD),
                                 jnp.concatenate([wv_, wv_], axis=-1).reshape(d, nk * LANES)],
                                axis=1).astype(BF16)
            gq, gk = _pad_vec(gqa_g_qn[j], LOG2E * GQA_HD ** -0.5), _pad_vec(gqa_g_kn[j])
            q, k, v, k_c, v_c = _qkv_proj(
                x, mod, g_mix_i, w, _gain_t(gq), gk, *tab_64, i,
                nq=nq, nk=nk, nv=nk * LANES, d_real=GQA_HD, half=GQA_HD // 2)
            rows = DEC_BATCH * PAST_LEN
            kc = _pad_rows_to_heads(cache_gqa_k[:, j].reshape(rows, nk * GQA_HD), nk, GQA_HD).astype(BF16)
            cv = cache_gqa_v[:, j].reshape(rows, nk, GQA_HD)
            vc = jnp.concatenate([cv, cv], axis=-1).reshape(rows, nk * LANES).astype(BF16).T
            o_p, o_s = _attention(q, k, v, kc, vc, (), mode="pair", k_shared=True, lam_init=0.0)
            w_o = gqa_w_o[j].astype(BF16)
            new_gqa.append((k_c[:T_PROMPT].reshape(BATCH, SEQ, nk, LANES)[..., :GQA_HD],
                            v_c[:T_PROMPT].reshape(BATCH, SEQ, nk, LANES)[..., :GQA_HD]))
        else:
            nh = DIFF_HEADS
            lam_init = 0.8 - 0.6 * math.exp(-0.3 * i)
            wq_, wk_, wv_ = jnp.split(diff_w_qkv[j], 3, axis=1)
            w = jnp.concatenate([_pad_heads(wq_, 2 * nh, DIFF_HD), _pad_heads(wk_, 2 * nh, DIFF_HD),
                                 wv_], axis=1).astype(BF16)
            gq, gk = _pad_vec(diff_g_qn[j], LOG2E * DIFF_HD ** -0.5), _pad_vec(diff_g_kn[j])
            q, k, v, k_c, v_c = _qkv_proj(
                x, mod, g_mix_i, w, _gain_t(gq), gk, *tab_64, i,
                nq=2 * nh, nk=2 * nh, nv=nh * 2 * DIFF_HD, d_real=DIFF_HD, half=DIFF_HD // 2)
            rows = DEC_BATCH * PAST_LEN
            kc = _pad_rows_to_heads(cache_diff_k[:, j].reshape(rows, 2 * nh * DIFF_HD), 2 * nh,
                                    DIFF_HD).astype(BF16)
            vc = cache_diff_v[:, j].reshape(rows, nh * 2 * DIFF_HD).astype(BF16).T
            lam_p = jnp.concatenate([_pad_vec(diff_lam_q1[j]), _pad_vec(diff_lam_k1[j]),
                                     _pad_vec(diff_lam_q2[j]), _pad_vec(diff_lam_k2[j])], axis=0)
            o_p, o_s = _attention(q, k, v, kc, vc, (lam_p, diff_g_sub[j].reshape(1, LANES)),
                                  mode="diff", k_shared=False, lam_init=lam_init)
            w_o = diff_w_o[j].astype(BF16)
            new_diff.append((k_c[:T_PROMPT].reshape(BATCH, SEQ, nh, 2, LANES)[..., :DIFF_HD],
                             v_c[:T_PROMPT].reshape(BATCH, SEQ, nh, 2 * DIFF_HD)))

        x, h2, comb = _oproj(o_p, o_s, x, mod, w_o, g_ffn[i].reshape(1, d), wr, br, i)
        w2g = w2b[i].reshape(MOE_GROUPS, EXPERTS_PER_GROUP * MOE_DIM, d)
        if i + 1 < DEPTH:
            x = _moe(h2, x, mod, comb, tri, w1b[i], w3b[i], w2g, i)
        else:
            y_prompt = _moe(h2, x, mod, comb, tri, w1b[i], w3b[i], w2g, i, 0, T_PROMPT)
            y_sample = _moe(h2, x, mod, comb, tri, w1b[i], w3b[i], w2g, i, T_PROMPT, T_SAMPLE)

    y_prompt = y_prompt.reshape(BATCH, SEQ, d)
    y_sample = y_sample.reshape(DEC_BATCH, DEC_SEQ, d)
    stack = lambda items, k: jnp.stack([t[k] for t in items], axis=1)
    return (y_prompt, y_sample, stack(new_mla, 0), stack(new_mla, 1), stack(new_gqa, 0),
            stack(new_gqa, 1), stack(new_diff, 0), stack(new_diff, 1))
```

```python
import functools
import math

import jax
import jax.numpy as jnp
from jax import lax
from jax.experimental import pallas as pl
from jax.experimental.pallas import tpu as pltpu

F32 = jnp.float32
BF16 = jnp.bfloat16

D_MODEL = 1024
BATCH, SEQ = 16, 256
DEC_BATCH, DEC_SEQ = 2, 4096
PAST_LEN = 512
DEPTH = 4
GRID_W = 64
EPS = 1e-6
ROPE_BASE = 10000.0
LOG2E = math.log2(math.e)
MLA_HEADS, MLA_NOPE, MLA_ROPE, MLA_QK, MLA_V = 16, 64, 32, 96, 64
MLA_Q_RANK, MLA_KV_RANK = 384, 256
GQA_Q_HEADS, GQA_KV_HEADS, GQA_HD = 16, 4, 64
DIFF_HEADS, DIFF_HD = 8, 64
MOE_EXPERTS, MOE_GROUPS, EXPERTS_PER_GROUP, MOE_DIM = 16, 4, 4, 256

T_PROMPT = BATCH * SEQ
T_SAMPLE = DEC_BATCH * DEC_SEQ
T_ALL = T_PROMPT + T_SAMPLE

LANES = 128
VMEM_LIMIT_BYTES = 56 * 1024 * 1024

TM_PROJ = 256
TM_MOE = 512
MOE_SUB = 160
TQ_SAMPLE = 1024
PROMPT_LOOKAHEAD = 4
Q_SPLIT = 4
ONES_ROWS = 16
CK = 512
MOD_NT = 1536
MOD_ROWS = 8


def _cparams(*sem):
    return pltpu.CompilerParams(dimension_semantics=sem, vmem_limit_bytes=VMEM_LIMIT_BYTES)


def _dot(a, b):
    return jnp.dot(a, b, preferred_element_type=F32)


def _rms(x, g):
    ms = jnp.mean(x * x, axis=-1, keepdims=True)
    return x * lax.rsqrt(ms + EPS) * g


def _norm_mod(x, g, scale, shift):
    return _rms(x, g) * (1.0 + scale) + shift


def _silu(x):
    return x / (1.0 + jnp.exp(-x))


def _mod_slices(m):
    d = D_MODEL
    return [m[:, k * d:(k + 1) * d] for k in range(6)]


def _mod_kernel(c_ref, w_ref, b_ref, o_ref):
    s = _silu(c_ref[...]).astype(BF16)
    o_ref[0] = _dot(s, w_ref[0].astype(BF16)) + b_ref[0]


def _modulation(c_all, w_mod, b_mod):
    n = 6 * D_MODEL
    return pl.pallas_call(
        _mod_kernel,
        out_shape=jax.ShapeDtypeStruct((DEPTH, MOD_ROWS, n), F32),
        grid=(DEPTH, n // MOD_NT),
        in_specs=[
            pl.BlockSpec((MOD_ROWS, D_MODEL), lambda l, j: (0, 0)),
            pl.BlockSpec((1, D_MODEL, MOD_NT), lambda l, j: (l, 0, j)),
            pl.BlockSpec((1, 1, MOD_NT), lambda l, j: (l, 0, j)),
        ],
        out_specs=pl.BlockSpec((1, MOD_ROWS, MOD_NT), lambda l, j: (l, 0, j)),
        compiler_params=_cparams("arbitrary", "arbitrary"),
        name="modulation",
    )(c_all, w_mod, b_mod.reshape(DEPTH, 1, n))


def _mod_row(i, tm, layer):
    r = jnp.where(i * tm < T_PROMPT, 0, 1 + (i * tm - T_PROMPT) // DEC_SEQ)
    return layer * MOD_ROWS + r


def _table_blk(i, tm):
    npt = T_PROMPT // tm
    return jnp.where(i < npt, i, npt + (i - npt) % (DEC_SEQ // tm))


def _row_specs(x, tm):
    if not isinstance(x, tuple):
        return [pl.BlockSpec((tm, D_MODEL), lambda i: (i, 0))]
    npt = T_PROMPT // tm
    return [pl.BlockSpec((tm, D_MODEL), lambda i: (jnp.minimum(i, npt - 1), 0)),
            pl.BlockSpec((tm, D_MODEL), lambda i: (jnp.maximum(i - npt, 0), 0))]


def _read_rows(refs, tm):
    if len(refs) == 1:
        return refs[0][...]
    return jnp.where(pl.program_id(0) < T_PROMPT // tm, refs[0][...], refs[1][...])


def _cache_blk(i, tm):
    return jnp.minimum(i, T_PROMPT // tm)


def _q_epilogue(y, g_t, tabs, d_real, half):
    ct, sat, sbt = tabs
    yt = y.T
    ms = jnp.sum(yt * yt, axis=0, keepdims=True) * (1.0 / d_real)
    qn = yt * lax.rsqrt(ms + EPS) * g_t
    up = jnp.concatenate([qn[half:], qn[:half]], axis=0)
    dn = jnp.concatenate([qn[LANES - half:], qn[:LANES - half]], axis=0)
    return qn * ct + up * sat + dn * sbt


def _k_epilogue(y, g, tabs, d_real, half):
    ms = jnp.sum(y * y, axis=-1, keepdims=True) * (1.0 / d_real)
    kc = y * lax.rsqrt(ms + EPS) * g
    if tabs is None:
        return kc, kc
    c, sa, sb = tabs
    return kc, kc * c + pltpu.roll(kc, LANES - half, 1) * sa + pltpu.roll(kc, half, 1) * sb


def _run_segments(segments, epilogue):
    prev = None
    for idx, (lhs, w_ref, col) in enumerate(segments):
        y = _dot(lhs, w_ref[:, col:col + 2 * LANES])
        if prev is not None:
            epilogue(*prev)
        prev = (idx, y)
    epilogue(*prev)


def _pack_halves(a, b, rotate):
    lane = lax.broadcasted_iota(jnp.int32, a.shape, 1)
    return jnp.where(lane < LANES // 2, a, pltpu.roll(b, LANES // 2, 1) if rotate else b)


def _store_head_blocks(idx, y, nq, nk, qtabs, ktabs, gq_t, gk, d_real, half,
                       qt_ref, k_ref, vt_ref, kc_ref, vc_ref, v_twice=False):
    j = 2 * idx
    blks = [y[:, :LANES], y[:, LANES:]]
    assert kc_ref is None or d_real == LANES // 2
    if j < nq:
        for hb in range(2):
            qt_ref[(j + hb) * LANES:(j + hb + 1) * LANES, :] = _q_epilogue(
                blks[hb], gq_t, qtabs, d_real, half).astype(BF16)
    elif j < nq + nk:
        j -= nq
        res = [_k_epilogue(blk, gk, ktabs, d_real, half) for blk in blks]
        for hb in range(2):
            k_ref[:, (j + hb) * LANES:(j + hb + 1) * LANES] = res[hb][1].astype(BF16)
        if kc_ref is not None:
            kc_ref[:, (j // 2) * LANES:(j // 2 + 1) * LANES] = _pack_halves(
                res[0][0], res[1][0], rotate=True)
    else:
        j -= nq + nk
        if vc_ref is not None and v_twice:
            vc_ref[:, (j // 2) * LANES:(j // 2 + 1) * LANES] = _pack_halves(*blks, rotate=False)
        elif vc_ref is not None:
            vc_ref[:, j * LANES:(j + 2) * LANES] = y
        for hb in range(2):
            vt_ref[(j + hb) * LANES:(j + hb + 1) * LANES, :] = blks[hb].T.astype(BF16)


def _qkv_kernel(x_ref, mod_ref, g_ref, w_ref, gq_ref, gk_ref, qc_ref, qsa_ref, qsb_ref,
                kc_t_ref, ksa_ref, ksb_ref, qt_ref, k_ref, vt_ref, kc_ref, vc_ref,
                *, nq, nk, d_real, half, v_twice):
    sh, sc = _mod_slices(mod_ref[0])[:2]
    h = _norm_mod(x_ref[...], g_ref[...], sc, sh).astype(BF16)
    qtabs = (qc_ref[...], qsa_ref[...], qsb_ref[...])
    ktabs = (kc_t_ref[...], ksa_ref[...], ksb_ref[...])
    gk = gk_ref[...]
    epilogue = functools.partial(
        _store_head_blocks, nq=nq, nk=nk, qtabs=qtabs, ktabs=ktabs, gq_t=gq_ref[...], gk=gk,
        d_real=d_real,
        half=half, qt_ref=qt_ref, k_ref=k_ref, vt_ref=vt_ref, kc_ref=kc_ref, vc_ref=vc_ref,
        v_twice=v_twice)
    n_seg = w_ref.shape[1] // (2 * LANES)
    _run_segments([(h, w_ref, s * 2 * LANES) for s in range(n_seg)], epilogue)


def _table_specs(tm):
    qtab = pl.BlockSpec((LANES, tm), lambda i: (0, _table_blk(i, tm)))
    ktab = pl.BlockSpec((tm, LANES), lambda i: (_table_blk(i, tm), 0))
    return [qtab, qtab, qtab, ktab, ktab, ktab]


def _qkv_proj(x, mod, g, w, gq_t, gk, qtabs, ktabs, layer, *, nq, nk, nv, d_real, half,
              v_twice):
    tm = TM_PROJ
    n = w.shape[1]
    wq, wk = nq * LANES, nk * LANES
    wkc = wk // 2 if d_real == LANES // 2 else wk
    wvc = nv // 2 if v_twice else nv
    row = lambda i: (i, 0)
    col = lambda i: (0, i)
    const = lambda i: (0, 0)
    cache = lambda width: pl.BlockSpec((tm, width), lambda i: (_cache_blk(i, tm), 0))
    return pl.pallas_call(
        functools.partial(_qkv_kernel, nq=nq, nk=nk, d_real=d_real, half=half, v_twice=v_twice),
        out_shape=(
            jax.ShapeDtypeStruct((wq, T_ALL), BF16),
            jax.ShapeDtypeStruct((T_ALL, wk), BF16),
            jax.ShapeDtypeStruct((nv, T_ALL), BF16),
            jax.ShapeDtypeStruct((T_PROMPT + tm, wkc), F32),
            jax.ShapeDtypeStruct((T_PROMPT + tm, wvc), F32),
        ),
        grid=(T_ALL // tm,),
        in_specs=[
            pl.BlockSpec((tm, D_MODEL), row),
            pl.BlockSpec((1, 1, 6 * D_MODEL), lambda i: (_mod_row(i, tm, layer), 0, 0)),
            pl.BlockSpec((1, D_MODEL), const),
            pl.BlockSpec((D_MODEL, n), const),
            pl.BlockSpec((LANES, tm), const),
            pl.BlockSpec((1, LANES), const),
        ] + _table_specs(tm),
        out_specs=(
            pl.BlockSpec((wq, tm), col),
            pl.BlockSpec((tm, wk), row),
            pl.BlockSpec((nv, tm), col),
            cache(wkc),
            cache(wvc),
        ),
        compiler_params=_cparams("arbitrary"),
        name="qkv_proj",
    )(x, mod, g, w, gq_t, gk, *qtabs, *ktabs)


def _down_kernel(*refs, tm):
    mod_ref, g_ref, w_ref, y_ref = refs[-4:]
    sh, sc = _mod_slices(mod_ref[0])[:2]
    h = _norm_mod(_read_rows(refs[:-4], tm), g_ref[...], sc, sh).astype(BF16)
    y_ref[...] = _dot(h, w_ref[...])


def _mla_down(x, mod, g, w, layer):
    tm = TM_PROJ
    n = w.shape[1]
    xs = x if isinstance(x, tuple) else (x,)
    return pl.pallas_call(
        functools.partial(_down_kernel, tm=tm),
        out_shape=jax.ShapeDtypeStruct((T_ALL, n), F32),
        grid=(T_ALL // tm,),
        in_specs=_row_specs(x, tm) + [
            pl.BlockSpec((1, 1, 6 * D_MODEL), lambda i: (_mod_row(i, tm, layer), 0, 0)),
            pl.BlockSpec((1, D_MODEL), lambda i: (0, 0)),
            pl.BlockSpec((D_MODEL, n), lambda i: (0, 0)),
        ],
        out_specs=pl.BlockSpec((tm, n), lambda i: (i, 0)),
        compiler_params=_cparams("arbitrary"),
        name="mla_down",
    )(*xs, mod, g, w)


MLA_NQ = MLA_HEADS
MLA_NV = MLA_HEADS * MLA_V // LANES


def _mla_up_kernel(y_ref, gq_ref, wuq_ref, gkv_ref, wk_ref, wv_ref, gqn_ref, gkn_ref,
                   qc_ref, qsa_ref, qsb_ref, kc_t_ref, ksa_ref, ksb_ref,
                   qt_ref, k_ref, vt_ref, ckv_ref, kpe_ref):
    y = y_ref[...]
    cqn = _rms(y[:, :MLA_Q_RANK], gq_ref[...]).astype(BF16)
    ckvn = _rms(y[:, MLA_Q_RANK:MLA_Q_RANK + MLA_KV_RANK], gkv_ref[...])
    kpe_blk = y[:, MLA_Q_RANK + MLA_KV_RANK:]
    ckv_ref[...] = ckvn
    kpe_ref[...] = kpe_blk
    ckv_b = ckvn.astype(BF16)
    kcat = jnp.concatenate([ckv_b, kpe_blk.astype(BF16)], axis=-1)
    qtabs = (qc_ref[...], qsa_ref[...], qsb_ref[...])
    ktabs = (kc_t_ref[...], ksa_ref[...], ksb_ref[...])
    epilogue = functools.partial(
        _store_head_blocks, nq=MLA_NQ, nk=MLA_HEADS, qtabs=qtabs, ktabs=ktabs,
        gq_t=gqn_ref[...], gk=gkn_ref[...],
        d_real=MLA_QK, half=MLA_ROPE // 2, qt_ref=qt_ref, k_ref=k_ref, vt_ref=vt_ref,
        kc_ref=None, vc_ref=None)
    seg = 2 * LANES
    segments = ([(cqn, wuq_ref, s * seg) for s in range(MLA_NQ // 2)]
                + [(kcat, wk_ref, s * seg) for s in range(MLA_HEADS // 2)]
                + [(ckv_b, wv_ref, s * seg) for s in range(MLA_NV // 2)])
    _run_segments(segments, epilogue)


def _mla_up(y1, gq, wuq, gkv, wk, wv, gqn_t, gkn, qtabs, ktabs):
    tm = TM_PROJ
    hq = MLA_HEADS * LANES
    hv = MLA_HEADS * MLA_V
    row = lambda i: (i, 0)
    col = lambda i: (0, i)
    const = lambda i: (0, 0)
    cache = lambda width: pl.BlockSpec((tm, width), lambda i: (_cache_blk(i, tm), 0))
    return pl.pallas_call(
        _mla_up_kernel,
        out_shape=(
            jax.ShapeDtypeStruct((hq, T_ALL), BF16),
            jax.ShapeDtypeStruct((T_ALL, hq), BF16),
            jax.ShapeDtypeStruct((hv, T_ALL), BF16),
            jax.ShapeDtypeStruct((T_PROMPT + tm, MLA_KV_RANK), F32),
            jax.ShapeDtypeStruct((T_PROMPT + tm, LANES), F32),
        ),
        grid=(T_ALL // tm,),
        in_specs=[
            pl.BlockSpec((tm, y1.shape[1]), row),
            pl.BlockSpec((1, MLA_Q_RANK), const),
            pl.BlockSpec(wuq.shape, const),
            pl.BlockSpec((1, MLA_KV_RANK), const),
            pl.BlockSpec(wk.shape, const),
            pl.BlockSpec(wv.shape, const),
            pl.BlockSpec((LANES, tm), const),
            pl.BlockSpec((1, LANES), const),
        ] + _table_specs(tm),
        out_specs=(
            pl.BlockSpec((hq, tm), col),
            pl.BlockSpec((tm, hq), row),
            pl.BlockSpec((hv, tm), col),
            cache(MLA_KV_RANK),
            cache(LANES),
        ),
        compiler_params=_cparams("arbitrary"),
        name="mla_up",
    )(y1, gq, wuq, gkv, wk, wv, gqn_t, gkn, *qtabs, *ktabs)


def _mla_cache_kernel(ckv_ref, kpe_ref, wk_ref, wv_ref, gkn_ref, k_ref, vt_ref):
    ckv_b = ckv_ref[...].astype(BF16)
    kcat = jnp.concatenate([ckv_b, kpe_ref[...].astype(BF16)], axis=-1)
    epilogue = functools.partial(
        _store_head_blocks, nq=0, nk=MLA_HEADS, qtabs=None, ktabs=None, gq_t=None,
        gk=gkn_ref[...],
        d_real=MLA_QK, half=MLA_ROPE // 2, qt_ref=None, k_ref=k_ref, vt_ref=vt_ref,
        kc_ref=None, vc_ref=None)
    seg = 2 * LANES
    segments = ([(kcat, wk_ref, s * seg) for s in range(MLA_HEADS // 2)]
                + [(ckv_b, wv_ref, s * seg) for s in range(MLA_NV // 2)])
    _run_segments(segments, epilogue)


def _mla_cache_expand(ckv, kpe_blk, wk, wv, gkn):
    rows = ckv.shape[0]
    tm = TM_PROJ
    hq = MLA_HEADS * LANES
    hv = MLA_HEADS * MLA_V
    row = lambda i: (i, 0)
    const = lambda i: (0, 0)
    return pl.pallas_call(
        _mla_cache_kernel,
        out_shape=(jax.ShapeDtypeStruct((rows, hq), BF16), jax.ShapeDtypeStruct((hv, rows), BF16)),
        grid=(rows // tm,),
        in_specs=[
            pl.BlockSpec((tm, MLA_KV_RANK), row),
            pl.BlockSpec((tm, LANES), row),
            pl.BlockSpec(wk.shape, const),
            pl.BlockSpec(wv.shape, const),
            pl.BlockSpec((1, LANES), const),
        ],
        out_specs=(pl.BlockSpec((tm, hq), row), pl.BlockSpec((hv, tm), lambda i: (0, i))),
        compiler_params=_cparams("arbitrary"),
        name="mla_cache_expand",
    )(ckv, kpe_blk, wk, wv, gkn)


def _attend(s, vt, state):
    m_cur = jnp.max(s, axis=0, keepdims=True)
    if state is None:
        return m_cur, _dot(vt, jnp.exp2(s - m_cur).astype(BF16))
    m, acc = state
    m_new = jnp.maximum(m, m_cur)
    alpha = jnp.exp2(m - m_new)
    return m_new, alpha * acc + _dot(vt, jnp.exp2(s - m_new).astype(BF16))


def _attn_kernel(*refs, mode, n_blocks, k_shared, has_cache, n_keys, ck, lam_init):
    refs = list(refs)
    qt_ref, k_ref, vt_ref = refs[:3]
    pos = 3
    if has_cache:
        kc_ref, vct_ref = refs[pos:pos + 2]
        pos += 2
    if mode == "diff":
        lam_ref, gsub_ref = refs[pos:pos + 2]
        pos += 2
        lp = lam_ref[...]
        lam = (jnp.exp(jnp.sum(lp[0:1] * lp[1:2], axis=-1, keepdims=True))
               - jnp.exp(jnp.sum(lp[2:3] * lp[3:4], axis=-1, keepdims=True)) + lam_init)
    o_ref = refs[pos]

    chunks = []
    if has_cache:
        chunks += [(kc_ref, vct_ref, 0, PAST_LEN)]
    chunks += [(k_ref, vt_ref, c * ck, ck) for c in range(n_keys // ck)]

    n_sub = Q_SPLIT if has_cache else 1
    w = qt_ref.shape[1] // n_sub
    streams = []
    for jb in range(n_blocks):
        if k_shared:
            k_sls = [slice((jb // 2) * LANES, (jb // 2 + 1) * LANES)] * 2
            v_sl = k_sls[0]
        else:
            k_sls = [slice((2 * jb + t) * LANES, (2 * jb + t + 1) * LANES) for t in range(2)]
            v_sl = slice(jb * LANES, (jb + 1) * LANES)
        for sub in range(n_sub):
            for t in range(2):
                q_sl = (slice((2 * jb + t) * LANES, (2 * jb + t + 1) * LANES),
                        slice(sub * w, (sub + 1) * w))
                streams.append((q_sl, k_sls[t], v_sl))

    def scores(item):
        x, c = item
        q_sl, k_sl, _ = streams[x]
        kr, _, start, n_ = chunks[c]
        return _dot(kr[start:start + n_, k_sl], qt_ref[q_sl])

    items = [(x, c) for c in range(len(chunks)) for x in range(len(streams))]
    ahead = len(streams) if has_cache else PROMPT_LOOKAHEAD
    sc = {k: scores(items[k]) for k in range(min(ahead, len(items)))}
    st = [None] * len(streams)
    for k, (x, c) in enumerate(items):
        _, vr, start, n_ = chunks[c]
        ones = (lax.broadcasted_iota(jnp.int32, (ONES_ROWS, n_), 0) == 0).astype(BF16)
        vt_ext = jnp.concatenate([vr[streams[x][2], start:start + n_], ones], axis=0)
        st[x] = _attend(sc.pop(k), vt_ext, st[x])
        if k + ahead < len(items):
            sc[k + ahead] = scores(items[k + ahead])

    for jb in range(n_blocks):
        base = jb * n_sub * 2
        cat = lambda t: jnp.concatenate(
            [st[base + 2 * sub + t][1] for sub in range(n_sub)], axis=1)
        acc_a, acc_b = cat(0), cat(1)
        oa = acc_a[:LANES] * (1.0 / acc_a[LANES:LANES + 1])
        ob = acc_b[:LANES] * (1.0 / acc_b[LANES:LANES + 1])
        if mode == "pair":
            row = lax.broadcasted_iota(jnp.int32, oa.shape, 0)
            o = jnp.where(row < LANES // 2, oa, ob).T
        else:
            o = oa - lam * ob
            ms = jnp.mean(o * o, axis=0, keepdims=True)
            o = (o * lax.rsqrt(ms + EPS)).T * (gsub_ref[...] * (1.0 - lam_init))
        o_ref[:, jb * LANES:(jb + 1) * LANES] = o.astype(BF16)


def _attention(qt, k, vt, kc, vct, extra, *, mode, k_shared, lam_init):
    n_blk = D_MODEL // LANES
    extra_specs_1 = [pl.BlockSpec(e.shape, lambda b: (0, 0)) for e in extra]
    extra_specs_3 = [pl.BlockSpec(e.shape, lambda b, j, i: (0, 0)) for e in extra]

    o_p = pl.pallas_call(
        functools.partial(_attn_kernel, mode=mode, n_blocks=n_blk, k_shared=k_shared,
                          has_cache=False, n_keys=SEQ, ck=SEQ, lam_init=lam_init),
        out_shape=jax.ShapeDtypeStruct((T_PROMPT, D_MODEL), BF16),
        grid=(BATCH,),
        in_specs=[
            pl.BlockSpec((qt.shape[0], SEQ), lambda b: (0, b)),
            pl.BlockSpec((SEQ, k.shape[1]), lambda b: (b, 0)),
            pl.BlockSpec((vt.shape[0], SEQ), lambda b: (0, b)),
        ] + extra_specs_1,
        out_specs=pl.BlockSpec((SEQ, D_MODEL), lambda b: (b, 0)),
        compiler_params=_cparams("arbitrary"),
        name="attn_prompt",
    )(qt, k, vt, *extra)

    tq = TQ_SAMPLE
    kv_col = (lambda j: j // 2) if k_shared else (lambda j: j)
    kw = LANES if k_shared else 2 * LANES
    lat = T_PROMPT // DEC_SEQ
    o_s = pl.pallas_call(
        functools.partial(_attn_kernel, mode=mode, n_blocks=1, k_shared=k_shared,
                          has_cache=True, n_keys=DEC_SEQ, ck=CK, lam_init=lam_init),
        out_shape=jax.ShapeDtypeStruct((T_SAMPLE, D_MODEL), BF16),
        grid=(DEC_BATCH, n_blk, DEC_SEQ // tq),
        in_specs=[
            pl.BlockSpec((2 * LANES, tq),
                         lambda b, j, i: (j, T_PROMPT // tq + b * (DEC_SEQ // tq) + i)),
            pl.BlockSpec((DEC_SEQ, kw), lambda b, j, i: (lat + b, kv_col(j))),
            pl.BlockSpec((LANES, DEC_SEQ), lambda b, j, i: (kv_col(j), lat + b)),
            pl.BlockSpec((PAST_LEN, kw), lambda b, j, i: (b, kv_col(j))),
            pl.BlockSpec((LANES, PAST_LEN), lambda b, j, i: (kv_col(j), b)),
        ] + extra_specs_3,
        out_specs=pl.BlockSpec((tq, LANES), lambda b, j, i: (b * (DEC_SEQ // tq) + i, j)),
        compiler_params=_cparams("arbitrary", "arbitrary", "arbitrary"),
        name="attn_latent",
    )(qt, k, vt, kc, vct, *extra)
    return o_p, o_s


def _route(sel_t, sc_t, tm):
    sel = [sel_t[e:e + 1, :] for e in range(MOE_EXPERTS)]
    sc = [sc_t[e:e + 1, :] for e in range(MOE_EXPERTS)]
    gscore = []
    for g in range(MOE_GROUPS):
        a, b, c, d = sel[4 * g:4 * g + 4]
        hi1, lo1 = jnp.maximum(a, b), jnp.minimum(a, b)
        hi2, lo2 = jnp.maximum(c, d), jnp.minimum(c, d)
        top1 = jnp.maximum(hi1, hi2)
        top2 = jnp.maximum(jnp.minimum(hi1, hi2), jnp.maximum(lo1, lo2))
        gscore.append(top1 + top2)
    gmax = jnp.maximum(jnp.maximum(gscore[0], gscore[1]), jnp.maximum(gscore[2], gscore[3]))
    taken = jnp.zeros_like(gmax)
    gsel = []
    for g in range(MOE_GROUPS):
        hit = jnp.where(gscore[g] == gmax, 1.0, 0.0) * (1.0 - taken)
        gsel.append(hit)
        taken = taken + hit
    vs, ss = [], []
    for e in range(EXPERTS_PER_GROUP):
        vs.append(sum(gsel[g] * sel[4 * g + e] for g in range(MOE_GROUPS)))
        ss.append(sum(gsel[g] * sc[4 * g + e] for g in range(MOE_GROUPS)))
    ws = []
    for i in range(EXPERTS_PER_GROUP):
        beaten = jnp.zeros_like(gmax)
        for j in range(EXPERTS_PER_GROUP):
            if j < i:
                beaten = beaten + jnp.where(vs[j] >= vs[i], 1.0, 0.0)
            elif j > i:
                beaten = beaten + jnp.where(vs[j] > vs[i], 1.0, 0.0)
        ws.append(jnp.where(beaten < 2.0, ss[i], 0.0))
    denom = (ws[0] + ws[1]) + (ws[2] + ws[3])
    inv = 1.0 / denom
    n_rows = 2 * MOE_EXPERTS
    row = lax.broadcasted_iota(jnp.int32, (n_rows, tm), 0)
    comb = jnp.zeros((n_rows, tm), F32)
    for g in range(MOE_GROUPS):
        comb = jnp.where(row == MOE_EXPERTS + g, jnp.broadcast_to(gsel[g], (n_rows, tm)), comb)
        for e in range(EXPERTS_PER_GROUP):
            w = gsel[g] * ws[e] * inv
            comb = jnp.where(row == 4 * g + e, jnp.broadcast_to(w, (n_rows, tm)), comb)
    return comb


def _oproj_kernel(*refs, tm):
    op_ref, os_ref = refs[:2]
    mod_ref, wo_ref, g_ref, wr_ref, br_ref, xo_ref, h_ref, comb_ref = refs[-8:]
    _, _, gt1, sh2, sc2, _ = _mod_slices(mod_ref[0])
    o = _read_rows((op_ref, os_ref), tm)
    xn = _read_rows(refs[2:-8], tm) + gt1 * _dot(o, wo_ref[...])
    xo_ref[...] = xn
    h = _norm_mod(xn, g_ref[...], sc2, sh2)
    hi = h.astype(BF16)
    lo = (h - hi.astype(F32)).astype(BF16)
    h_ref[...] = hi
    wr = wr_ref[...]
    r1 = _dot(hi, wr)
    logits = r1[:, :LANES] + r1[:, LANES:] + _dot(lo, wr[:, :LANES])
    scores = 1.0 / (1.0 + jnp.exp(-logits))
    sel = scores + br_ref[...]
    comb = _route(sel.T, scores.T, tm)
    comb = jnp.concatenate([comb, jnp.zeros((LANES - 2 * MOE_EXPERTS, tm), F32)], axis=0)
    comb_ref[...] = comb.T


def _oproj(o_p, o_s, x, mod, wo, g_ffn, wr, br, layer):
    tm = TM_PROJ
    row = lambda i: (i, 0)
    const = lambda i: (0, 0)
    xs = x if isinstance(x, tuple) else (x,)
    return pl.pallas_call(
        functools.partial(_oproj_kernel, tm=tm),
        out_shape=(
            jax.ShapeDtypeStruct((T_ALL, D_MODEL), F32),
            jax.ShapeDtypeStruct((T_ALL, D_MODEL), BF16),
            jax.ShapeDtypeStruct((T_ALL, LANES), F32),
        ),
        grid=(T_ALL // tm,),
        in_specs=_row_specs((o_p, o_s), tm) + _row_specs(x, tm) + [
            pl.BlockSpec((1, 1, 6 * D_MODEL), lambda i: (_mod_row(i, tm, layer), 0, 0)),
            pl.BlockSpec((D_MODEL, D_MODEL), const),
            pl.BlockSpec((1, D_MODEL), const),
            pl.BlockSpec((D_MODEL, 2 * LANES), const),
            pl.BlockSpec((1, LANES), const),
        ],
        out_specs=(
            pl.BlockSpec((tm, D_MODEL), row),
            pl.BlockSpec((tm, D_MODEL), row),
            pl.BlockSpec((tm, LANES), row),
        ),
        compiler_params=_cparams("arbitrary"),
        name="oproj_router",
    )(o_p, o_s, *xs, mod, wo, g_ffn, wr, br)


def _moe_kernel(h_ref, x_ref, mod_ref, comb_ref, tri_ref, w1_ref, w3_ref, w2_ref, o_ref, acc_ref):
    gt2 = _mod_slices(mod_ref[0])[5]
    comb = comb_ref[...]
    comb_hi = comb.astype(BF16)
    comb_lo = (comb - comb_hi.astype(F32)).astype(BF16)
    rank = _dot(tri_ref[...], comb_hi)
    rank_t, comb_t = rank.T, comb.T
    sub_r = lax.broadcasted_iota(jnp.int32, (MOE_SUB, 1), 0).astype(F32)
    sub_c = lax.broadcasted_iota(jnp.int32, (1, MOE_SUB), 1).astype(F32)

    slots, counts = [], []
    for g in range(MOE_GROUPS):
        lane = MOE_EXPERTS + g
        member_r = comb_t[lane:lane + 1, :]
        member_c = comb[:, lane:lane + 1]
        slots.append((jnp.where(member_r > 0.5, rank_t[lane:lane + 1, :], -1.0),
                      jnp.where(member_c > 0.5, rank[:, lane:lane + 1], -1.0)))
        counts.append(jnp.sum(member_r).astype(jnp.int32))

    def select(g, base):
        slot_r, slot_c = slots[g]
        sel = jnp.where(slot_r - base == sub_r, 1.0, 0.0).astype(BF16)
        sel_t = jnp.where(slot_c - base == sub_c, 1.0, 0.0).astype(BF16)
        return sel, sel_t

    def gather(sel):
        return _dot(sel, h_ref[...]).astype(BF16), _dot(sel, comb_hi) + _dot(sel, comb_lo)

    def experts(g, hc, wc):
        parts = []
        for e in range(EXPERTS_PER_GROUP):
            ex = EXPERTS_PER_GROUP * g + e
            a = _dot(hc, w1_ref[ex])
            u = _dot(hc, w3_ref[ex])
            parts.append((_silu(a) * u * wc[:, ex:ex + 1]).astype(BF16))
        return jnp.concatenate(parts, axis=-1)

    sels = [select(g, 0.0) for g in range(MOE_GROUPS)]
    packed = [gather(sel) for sel, _ in sels]
    hids = [experts(g, *packed[g]) for g in range(MOE_GROUPS)]
    outs = [_dot(hids[g], w2_ref[g]).astype(BF16) for g in range(MOE_GROUPS)]
    acc = _dot(sels[0][1], outs[0])
    for g in range(1, MOE_GROUPS):
        acc = acc + _dot(sels[g][1], outs[g])
    acc_ref[...] = acc

    for g in range(MOE_GROUPS):
        def body(b, carry):
            sel, sel_t = select(g, (b * MOE_SUB).astype(F32))
            hc, wc = gather(sel)
            og = _dot(experts(g, hc, wc), w2_ref[g]).astype(BF16)
            acc_ref[...] += _dot(sel_t, og)
            return carry

        lax.fori_loop(1, lax.div(counts[g] + (MOE_SUB - 1), MOE_SUB), body, 0)
    o_ref[...] = x_ref[...] + gt2 * acc_ref[...]


def _moe(h, x, mod, comb, tri, w1, w3, w2, layer, row0=0, n_rows=T_ALL):
    tm = TM_MOE
    t0 = row0 // tm
    row = lambda i: (t0 + i, 0)
    whole = lambda shape: pl.BlockSpec(shape, lambda i: (0, 0, 0), pipeline_mode=pl.Buffered(1))
    return pl.pallas_call(
        _moe_kernel,
        out_shape=jax.ShapeDtypeStruct((n_rows, D_MODEL), F32),
        grid=(n_rows // tm,),
        in_specs=[
            pl.BlockSpec((tm, D_MODEL), row),
            pl.BlockSpec((tm, D_MODEL), row),
            pl.BlockSpec((1, 1, 6 * D_MODEL), lambda i: (_mod_row(t0 + i, tm, layer), 0, 0)),
            pl.BlockSpec((tm, LANES), row),
            pl.BlockSpec((tm, tm), lambda i: (0, 0)),
            whole(w1.shape), whole(w3.shape), whole(w2.shape),
        ],
        out_specs=pl.BlockSpec((tm, D_MODEL), lambda i: (i, 0)),
        scratch_shapes=[pltpu.VMEM((tm, D_MODEL), F32)],
        compiler_params=_cparams("arbitrary"),
        name="moe",
    )(h, x, mod, comb, tri, w1, w3, w2)


def _rope_tables(d_rot, lane0):
    n = DEC_SEQ
    half = d_rot // 2
    n_freq = d_rot // 4
    t = jnp.arange(n)
    rowp = (t // GRID_W).astype(F32)
    colp = (t % GRID_W).astype(F32)
    inv = jnp.power(ROPE_BASE, -jnp.arange(n_freq, dtype=F32) / n_freq)
    ang = jnp.concatenate([rowp[:, None] * inv, colp[:, None] * inv], axis=-1)
    cos, sin = jnp.cos(ang), jnp.sin(ang)
    c = jnp.ones((n, LANES), F32).at[:, lane0:lane0 + d_rot].set(jnp.concatenate([cos, cos], -1))
    sa = jnp.zeros((n, LANES), F32).at[:, lane0:lane0 + half].set(-sin)
    sb = jnp.zeros((n, LANES), F32).at[:, lane0 + half:lane0 + d_rot].set(sin)
    ident = (jnp.ones((T_PROMPT, LANES), F32), jnp.zeros((T_PROMPT, LANES), F32),
             jnp.zeros((T_PROMPT, LANES), F32))
    tabs = tuple(jnp.concatenate([i_, t_], axis=0) for i_, t_ in zip(ident, (c, sa, sb)))
    return tuple(t_.T for t_ in tabs), tabs


def _gain_t(g):
    return jnp.broadcast_to(g.reshape(LANES, 1), (LANES, TM_PROJ))


def _pad_heads(w, n_heads, d):
    k = w.shape[0]
    w = w.reshape(k, n_heads, d)
    return jnp.pad(w, ((0, 0), (0, 0), (0, LANES - d))).reshape(k, n_heads * LANES)


def _pad_vec(g, scale=1.0):
    return jnp.pad(g.astype(F32) * scale, (0, LANES - g.shape[0])).reshape(1, LANES)


def _pad_rows_to_heads(x, n_heads, d):
    r = x.shape[0]
    x = x.reshape(r, n_heads, d)
    return jnp.pad(x, ((0, 0), (0, 0), (0, LANES - d))).reshape(r, n_heads * LANES)


def kernel(x_prompt, x_sample, cache_mla_ckv, cache_mla_kpe, cache_gqa_k, cache_gqa_v, cache_diff_k, cache_diff_v, c, c_ctx, g_mix, g_ffn, w_mod, b_mod, w_router, b_router, w_e1, w_e3, w_e2, mla_w_dq, mla_g_q, mla_w_uq, mla_w_dkv, mla_g_kv, mla_w_ukv, mla_g_qn, mla_g_kn, mla_w_o, gqa_w_qkv, gqa_g_qn, gqa_g_kn, gqa_w_o, diff_w_qkv, diff_g_qn, diff_g_kn, diff_lam_q1, diff_lam_k1, diff_lam_q2, diff_lam_k2, diff_g_sub, diff_w_o):
    d = D_MODEL
    x = (x_prompt.reshape(T_PROMPT, d), x_sample.reshape(T_SAMPLE, d))

    c_all = jnp.concatenate([c_ctx[None, :], c, jnp.zeros((MOD_ROWS - 1 - DEC_BATCH, d), F32)], axis=0)
    mod = _modulation(c_all, w_mod, b_mod).reshape(DEPTH * MOD_ROWS, 1, 6 * d)

    w_hi = w_router.astype(BF16)
    w_lo = (w_router - w_hi.astype(F32)).astype(BF16)
    pad_r = ((0, 0), (0, LANES - MOE_EXPERTS))
    wr = jnp.concatenate([jnp.pad(w_hi, pad_r), jnp.pad(w_lo, pad_r)], axis=1)
    br = _pad_vec(b_router)

    w1b, w3b, w2b = w_e1.astype(BF16), w_e3.astype(BF16), w_e2.astype(BF16)
    t_idx = jnp.arange(TM_MOE)
    tri = (t_idx[None, :] < t_idx[:, None]).astype(BF16)

    tab_mla = _rope_tables(MLA_ROPE, MLA_NOPE)
    tab_64 = _rope_tables(GQA_HD, 0)

    new_mla, new_gqa, new_diff = [], [], []
    for i in range(DEPTH):
        kind, j = i % 3, i // 3
        g_mix_i = g_mix[i].reshape(1, d)
        if kind == 0:
            w_down = jnp.concatenate(
                [mla_w_dq[j], mla_w_dkv[j],
                 jnp.zeros((d, LANES - MLA_ROPE), F32)], axis=1).astype(BF16)
            wuq = _pad_heads(mla_w_uq[j], MLA_HEADS, MLA_QK).astype(BF16)
            ukv = mla_w_ukv[j].reshape(MLA_KV_RANK, MLA_HEADS, MLA_NOPE + MLA_V)
            wk_nope = jnp.pad(ukv[:, :, :MLA_NOPE], ((0, 0), (0, 0), (0, LANES - MLA_NOPE)))
            place = jnp.zeros((LANES, MLA_HEADS, LANES), F32)
            r = jnp.arange(MLA_ROPE)
            place = place.at[r, :, MLA_NOPE + r].set(1.0)
            wk = jnp.concatenate([wk_nope.reshape(MLA_KV_RANK, -1), place.reshape(LANES, -1)],
                                 axis=0).astype(BF16)
            wv = ukv[:, :, MLA_NOPE:].reshape(MLA_KV_RANK, MLA_HEADS * MLA_V).astype(BF16)
            gqn = _pad_vec(mla_g_qn[j], LOG2E * MLA_QK ** -0.5)
            gkn = _pad_vec(mla_g_kn[j])
            y1 = _mla_down(x, mod, g_mix_i, w_down, i)
            q, k, v, ckv_c, kpe_c = _mla_up(
                y1, mla_g_q[j].reshape(1, -1), wuq, mla_g_kv[j].reshape(1, -1), wk, wv,
                _gain_t(gqn), gkn, *tab_mla)
            cache_kpe = jnp.pad(cache_mla_kpe[:, j].reshape(DEC_BATCH * PAST_LEN, MLA_ROPE),
                                ((0, 0), (0, LANES - MLA_ROPE)))
            kc, vc = _mla_cache_expand(cache_mla_ckv[:, j].reshape(DEC_BATCH * PAST_LEN, MLA_KV_RANK),
                                       cache_kpe, wk, wv, gkn)
            o_p, o_s = _attention(q, k, v, kc, vc, (), mode="pair", k_shared=False, lam_init=0.0)
            w_o = mla_w_o[j].astype(BF16)
            new_mla.append((ckv_c[:T_PROMPT].reshape(BATCH, SEQ, MLA_KV_RANK),
                            kpe_c[:T_PROMPT, :MLA_ROPE].reshape(BATCH, SEQ, MLA_ROPE)))
        elif kind == 1:
            nq, nk = GQA_Q_HEADS, GQA_KV_HEADS
            wq_, wk_, wv_ = jnp.split(gqa_w_qkv[j], [nq * GQA_HD, (nq + nk) * GQA_HD], axis=1)
            wv_ = wv_.reshape(d, nk, GQA_HD)
            w = jnp.concatenate([_pad_heads(wq_, nq, GQA_HD), _pad_heads(wk_, nk, GQA_HD),
                                 jnp.concatenate([wv_, wv_], axis=-1).reshape(d, nk * LANES)],
                                axis=1).astype(BF16)
            gq, gk = _pad_vec(gqa_g_qn[j], LOG2E * GQA_HD ** -0.5), _pad_vec(gqa_g_kn[j])
            q, k, v, k_c, v_c = _qkv_proj(
                x, mod, g_mix_i, w, _gain_t(gq), gk, *tab_64, i,
                nq=nq, nk=nk, nv=nk * LANES, d_real=GQA_HD, half=GQA_HD // 2, v_twice=True)
            rows = DEC_BATCH * PAST_LEN
            kc = _pad_rows_to_heads(cache_gqa_k[:, j].reshape(rows, nk * GQA_HD), nk, GQA_HD).astype(BF16)
            cv = cache_gqa_v[:, j].reshape(rows, nk, GQA_HD)
            vc = jnp.concatenate([cv, cv], axis=-1).reshape(rows, nk * LANES).astype(BF16).T
            o_p, o_s = _attention(q, k, v, kc, vc, (), mode="pair", k_shared=True, lam_init=0.0)
            w_o = gqa_w_o[j].astype(BF16)
            new_gqa.append((k_c[:T_PROMPT].reshape(BATCH, SEQ, nk, GQA_HD),
                            v_c[:T_PROMPT].reshape(BATCH, SEQ, nk, GQA_HD)))
        else:
            nh = DIFF_HEADS
            lam_init = 0.8 - 0.6 * math.exp(-0.3 * i)
            wq_, wk_, wv_ = jnp.split(diff_w_qkv[j], 3, axis=1)
            w = jnp.concatenate([_pad_heads(wq_, 2 * nh, DIFF_HD), _pad_heads(wk_, 2 * nh, DIFF_HD),
                                 wv_], axis=1).astype(BF16)
            gq, gk = _pad_vec(diff_g_qn[j], LOG2E * DIFF_HD ** -0.5), _pad_vec(diff_g_kn[j])
            q, k, v, k_c, v_c = _qkv_proj(
                x, mod, g_mix_i, w, _gain_t(gq), gk, *tab_64, i,
                nq=2 * nh, nk=2 * nh, nv=nh * 2 * DIFF_HD, d_real=DIFF_HD, half=DIFF_HD // 2,
                v_twice=False)
            rows = DEC_BATCH * PAST_LEN
            kc = _pad_rows_to_heads(cache_diff_k[:, j].reshape(rows, 2 * nh * DIFF_HD), 2 * nh,
                                    DIFF_HD).astype(BF16)
            vc = cache_diff_v[:, j].reshape(rows, nh * 2 * DIFF_HD).astype(BF16).T
            lam_p = jnp.concatenate([_pad_vec(diff_lam_q1[j]), _pad_vec(diff_lam_k1[j]),
                                     _pad_vec(diff_lam_q2[j]), _pad_vec(diff_lam_k2[j])], axis=0)
            o_p, o_s = _attention(q, k, v, kc, vc, (lam_p, diff_g_sub[j].reshape(1, LANES)),
                                  mode="diff", k_shared=False, lam_init=lam_init)
            w_o = diff_w_o[j].astype(BF16)
            new_diff.append((k_c[:T_PROMPT].reshape(BATCH, SEQ, nh, 2, DIFF_HD),
                             v_c[:T_PROMPT].reshape(BATCH, SEQ, nh, 2 * DIFF_HD)))

        x, h2, comb = _oproj(o_p, o_s, x, mod, w_o, g_ffn[i].reshape(1, d), wr, br, i)
        w2g = w2b[i].reshape(MOE_GROUPS, EXPERTS_PER_GROUP * MOE_DIM, d)
        if i + 1 < DEPTH:
            x = _moe(h2, x, mod, comb, tri, w1b[i], w3b[i], w2g, i)
        else:
            y_prompt = _moe(h2, x, mod, comb, tri, w1b[i], w3b[i], w2g, i, 0, T_PROMPT)
            y_sample = _moe(h2, x, mod, comb, tri, w1b[i], w3b[i], w2g, i, T_PROMPT, T_SAMPLE)

    y_prompt = y_prompt.reshape(BATCH, SEQ, d)
    y_sample = y_sample.reshape(DEC_BATCH, DEC_SEQ, d)
    stack = lambda items, k: jnp.stack([t[k] for t in items], axis=1)
    return (y_prompt, y_sample, stack(new_mla, 0), stack(new_mla, 1), stack(new_gqa, 0),
            stack(new_gqa, 1), stack(new_diff, 0), stack(new_diff, 1))
```

```python
import functools
import math

import jax
import jax.numpy as jnp
from jax import lax
from jax.experimental import pallas as pl
from jax.experimental.pallas import tpu as pltpu

F32 = jnp.float32
BF16 = jnp.bfloat16

D_MODEL = 1024
BATCH, SEQ = 16, 256
DEC_BATCH, DEC_SEQ = 2, 4096
PAST_LEN = 512
DEPTH = 4
GRID_W = 64
EPS = 1e-6
ROPE_BASE = 10000.0
LOG2E = math.log2(math.e)
MLA_HEADS, MLA_NOPE, MLA_ROPE, MLA_QK, MLA_V = 16, 64, 32, 96, 64
MLA_Q_RANK, MLA_KV_RANK = 384, 256
GQA_Q_HEADS, GQA_KV_HEADS, GQA_HD = 16, 4, 64
DIFF_HEADS, DIFF_HD = 8, 64
MOE_EXPERTS, MOE_GROUPS, EXPERTS_PER_GROUP, MOE_DIM = 16, 4, 4, 256

T_PROMPT = BATCH * SEQ
T_SAMPLE = DEC_BATCH * DEC_SEQ
T_ALL = T_PROMPT + T_SAMPLE

LANES = 128
VMEM_LIMIT_BYTES = 56 * 1024 * 1024

TM_PROJ = 256
TM_OPROJ = 1024
TM_MOE = 512
MOE_SUB = 160
TQ_SAMPLE = 2048
PROMPT_LOOKAHEAD = 4
Q_SPLIT = 8
ONES_ROWS = 16
CK = 512
MOD_NT = 1536
MOD_ROWS = 8


def _cparams(*sem):
    return pltpu.CompilerParams(dimension_semantics=sem, vmem_limit_bytes=VMEM_LIMIT_BYTES)


def _dot(a, b):
    return jnp.dot(a, b, preferred_element_type=F32)


def _rms(x, g):
    ms = jnp.mean(x * x, axis=-1, keepdims=True)
    return x * lax.rsqrt(ms + EPS) * g


def _norm_mod(x, g, scale, shift):
    return _rms(x, g) * (1.0 + scale) + shift


def _silu(x):
    return x / (1.0 + jnp.exp(-x))


def _mod_slices(m):
    d = D_MODEL
    return [m[:, k * d:(k + 1) * d] for k in range(6)]


def _mod_kernel(c_ref, w_ref, b_ref, o_ref):
    s = _silu(c_ref[...]).astype(BF16)
    o_ref[0] = _dot(s, w_ref[0].astype(BF16)) + b_ref[0]


def _modulation(c_all, w_mod, b_mod):
    n = 6 * D_MODEL
    return pl.pallas_call(
        _mod_kernel,
        out_shape=jax.ShapeDtypeStruct((DEPTH, MOD_ROWS, n), F32),
        grid=(DEPTH, n // MOD_NT),
        in_specs=[
            pl.BlockSpec((MOD_ROWS, D_MODEL), lambda l, j: (0, 0)),
            pl.BlockSpec((1, D_MODEL, MOD_NT), lambda l, j: (l, 0, j)),
            pl.BlockSpec((1, 1, MOD_NT), lambda l, j: (l, 0, j)),
        ],
        out_specs=pl.BlockSpec((1, MOD_ROWS, MOD_NT), lambda l, j: (l, 0, j)),
        compiler_params=_cparams("arbitrary", "arbitrary"),
        name="modulation",
    )(c_all, w_mod, b_mod.reshape(DEPTH, 1, n))


def _mod_row(i, tm, layer):
    r = jnp.where(i * tm < T_PROMPT, 0, 1 + (i * tm - T_PROMPT) // DEC_SEQ)
    return layer * MOD_ROWS + r


def _table_blk(i, tm):
    npt = T_PROMPT // tm
    return jnp.where(i < npt, i, npt + (i - npt) % (DEC_SEQ // tm))


def _row_specs(x, tm):
    if not isinstance(x, tuple):
        return [pl.BlockSpec((tm, D_MODEL), lambda i: (i, 0))]
    npt = T_PROMPT // tm
    return [pl.BlockSpec((tm, D_MODEL), lambda i: (jnp.minimum(i, npt - 1), 0)),
            pl.BlockSpec((tm, D_MODEL), lambda i: (jnp.maximum(i - npt, 0), 0))]


def _read_rows(refs, tm):
    if len(refs) == 1:
        return refs[0][...]
    return jnp.where(pl.program_id(0) < T_PROMPT // tm, refs[0][...], refs[1][...])


def _cache_blk(i, tm):
    return jnp.minimum(i, T_PROMPT // tm)


def _q_epilogue(y, g_t, tabs, d_real, half):
    ct, sat, sbt = tabs
    yt = y.T
    ms = jnp.sum(yt * yt, axis=0, keepdims=True) * (1.0 / d_real)
    qn = yt * lax.rsqrt(ms + EPS) * g_t
    up = jnp.concatenate([qn[half:], qn[:half]], axis=0)
    dn = jnp.concatenate([qn[LANES - half:], qn[:LANES - half]], axis=0)
    return qn * ct + up * sat + dn * sbt


def _k_epilogue(y, g, tabs, d_real, half):
    ms = jnp.sum(y * y, axis=-1, keepdims=True) * (1.0 / d_real)
    kc = y * lax.rsqrt(ms + EPS) * g
    if tabs is None:
        return kc, kc
    c, sa, sb = tabs
    return kc, kc * c + pltpu.roll(kc, LANES - half, 1) * sa + pltpu.roll(kc, half, 1) * sb


def _run_segments(segments, epilogue):
    prev = None
    for idx, (lhs, w_ref, col) in enumerate(segments):
        y = _dot(lhs, w_ref[:, col:col + 2 * LANES])
        if prev is not None:
            epilogue(*prev)
        prev = (idx, y)
    epilogue(*prev)


def _store_head_blocks(idx, y, nq, nk, qtabs, ktabs, gq_t, gk, d_real, half,
                       qt_ref, k_ref, vt_ref, kc_ref, vc_ref):
    j = 2 * idx
    blks = [y[:, :LANES], y[:, LANES:]]
    if j < nq:
        for hb in range(2):
            qt_ref[(j + hb) * LANES:(j + hb + 1) * LANES, :] = _q_epilogue(
                blks[hb], gq_t, qtabs, d_real, half).astype(BF16)
    elif j < nq + nk:
        j -= nq
        for hb in range(2):
            sl = slice((j + hb) * LANES, (j + hb + 1) * LANES)
            kc, kr = _k_epilogue(blks[hb], gk, ktabs, d_real, half)
            if kc_ref is not None:
                kc_ref[:, sl] = kc
            k_ref[:, sl] = kr.astype(BF16)
    else:
        j -= nq + nk
        if vc_ref is not None:
            vc_ref[:, j * LANES:(j + 2) * LANES] = y
        for hb in range(2):
            vt_ref[(j + hb) * LANES:(j + hb + 1) * LANES, :] = blks[hb].T.astype(BF16)


def _qkv_kernel(x_ref, mod_ref, g_ref, w_ref, gq_ref, gk_ref, qc_ref, qsa_ref, qsb_ref,
                kc_t_ref, ksa_ref, ksb_ref, qt_ref, k_ref, vt_ref, kc_ref, vc_ref,
                *, nq, nk, d_real, half):
    sh, sc = _mod_slices(mod_ref[0])[:2]
    h = _norm_mod(x_ref[...], g_ref[...], sc, sh).astype(BF16)
    qtabs = (qc_ref[...], qsa_ref[...], qsb_ref[...])
    ktabs = (kc_t_ref[...], ksa_ref[...], ksb_ref[...])
    gk = gk_ref[...]
    epilogue = functools.partial(
        _store_head_blocks, nq=nq, nk=nk, qtabs=qtabs, ktabs=ktabs, gq_t=gq_ref[...], gk=gk,
        d_real=d_real,
        half=half, qt_ref=qt_ref, k_ref=k_ref, vt_ref=vt_ref, kc_ref=kc_ref, vc_ref=vc_ref)
    n_seg = w_ref.shape[1] // (2 * LANES)
    _run_segments([(h, w_ref, s * 2 * LANES) for s in range(n_seg)], epilogue)


def _table_specs(tm):
    qtab = pl.BlockSpec((LANES, tm), lambda i: (0, _table_blk(i, tm)))
    ktab = pl.BlockSpec((tm, LANES), lambda i: (_table_blk(i, tm), 0))
    return [qtab, qtab, qtab, ktab, ktab, ktab]


def _qkv_proj(x, mod, g, w, gq_t, gk, qtabs, ktabs, layer, *, nq, nk, nv, d_real, half):
    tm = TM_PROJ
    n = w.shape[1]
    wq, wk = nq * LANES, nk * LANES
    row = lambda i: (i, 0)
    col = lambda i: (0, i)
    const = lambda i: (0, 0)
    cache = lambda width: pl.BlockSpec((tm, width), lambda i: (_cache_blk(i, tm), 0))
    return pl.pallas_call(
        functools.partial(_qkv_kernel, nq=nq, nk=nk, d_real=d_real, half=half),
        out_shape=(
            jax.ShapeDtypeStruct((wq, T_ALL), BF16),
            jax.ShapeDtypeStruct((T_ALL, wk), BF16),
            jax.ShapeDtypeStruct((nv, T_ALL), BF16),
            jax.ShapeDtypeStruct((T_PROMPT + tm, wk), F32),
            jax.ShapeDtypeStruct((T_PROMPT + tm, nv), F32),
        ),
        grid=(T_ALL // tm,),
        in_specs=[
            pl.BlockSpec((tm, D_MODEL), row),
            pl.BlockSpec((1, 1, 6 * D_MODEL), lambda i: (_mod_row(i, tm, layer), 0, 0)),
            pl.BlockSpec((1, D_MODEL), const),
            pl.BlockSpec((D_MODEL, n), const),
            pl.BlockSpec((LANES, tm), const),
            pl.BlockSpec((1, LANES), const),
        ] + _table_specs(tm),
        out_specs=(
            pl.BlockSpec((wq, tm), col),
            pl.BlockSpec((tm, wk), row),
            pl.BlockSpec((nv, tm), col),
            cache(wk),
            cache(nv),
        ),
        compiler_params=_cparams("arbitrary"),
        name="qkv_proj",
    )(x, mod, g, w, gq_t, gk, *qtabs, *ktabs)


MLA_NQ = MLA_HEADS
MLA_NV = MLA_HEADS * MLA_V // LANES


def _mla_kernel(*refs, tm):
    (mod_ref, g_ref, wd_ref, gq_ref, wuq_ref, gkv_ref, wk_ref, wv_ref, gqn_ref, gkn_ref,
     qc_ref, qsa_ref, qsb_ref, kc_t_ref, ksa_ref, ksb_ref,
     qt_ref, k_ref, vt_ref, ckv_ref, kpe_ref) = refs[-21:]
    sh, sc = _mod_slices(mod_ref[0])[:2]
    h = _norm_mod(_read_rows(refs[:-21], tm), g_ref[...], sc, sh).astype(BF16)
    y = _dot(h, wd_ref[...])
    cqn = _rms(y[:, :MLA_Q_RANK], gq_ref[...]).astype(BF16)
    ckvn = _rms(y[:, MLA_Q_RANK:MLA_Q_RANK + MLA_KV_RANK], gkv_ref[...])
    kpe_blk = y[:, MLA_Q_RANK + MLA_KV_RANK:]
    ckv_ref[...] = ckvn
    kpe_ref[...] = kpe_blk
    ckv_b = ckvn.astype(BF16)
    kcat = jnp.concatenate([ckv_b, kpe_blk.astype(BF16)], axis=-1)
    qtabs = (qc_ref[...], qsa_ref[...], qsb_ref[...])
    ktabs = (kc_t_ref[...], ksa_ref[...], ksb_ref[...])
    epilogue = functools.partial(
        _store_head_blocks, nq=MLA_NQ, nk=MLA_HEADS, qtabs=qtabs, ktabs=ktabs,
        gq_t=gqn_ref[...], gk=gkn_ref[...],
        d_real=MLA_QK, half=MLA_ROPE // 2, qt_ref=qt_ref, k_ref=k_ref, vt_ref=vt_ref,
        kc_ref=None, vc_ref=None)
    seg = 2 * LANES
    segments = ([(cqn, wuq_ref, s * seg) for s in range(MLA_NQ // 2)]
                + [(kcat, wk_ref, s * seg) for s in range(MLA_HEADS // 2)]
                + [(ckv_b, wv_ref, s * seg) for s in range(MLA_NV // 2)])
    _run_segments(segments, epilogue)


def _mla_proj(x, mod, g, w_down, gq, wuq, gkv, wk, wv, gqn_t, gkn, qtabs, ktabs, layer):
    tm = TM_PROJ
    xs = x if isinstance(x, tuple) else (x,)
    hq = MLA_HEADS * LANES
    hv = MLA_HEADS * MLA_V
    row = lambda i: (i, 0)
    col = lambda i: (0, i)
    const = lambda i: (0, 0)
    cache = lambda width: pl.BlockSpec((tm, width), lambda i: (_cache_blk(i, tm), 0))
    return pl.pallas_call(
        functools.partial(_mla_kernel, tm=tm),
        out_shape=(
            jax.ShapeDtypeStruct((hq, T_ALL), BF16),
            jax.ShapeDtypeStruct((T_ALL, hq), BF16),
            jax.ShapeDtypeStruct((hv, T_ALL), BF16),
            jax.ShapeDtypeStruct((T_PROMPT + tm, MLA_KV_RANK), F32),
            jax.ShapeDtypeStruct((T_PROMPT + tm, LANES), F32),
        ),
        grid=(T_ALL // tm,),
        in_specs=_row_specs(x, tm) + [
            pl.BlockSpec((1, 1, 6 * D_MODEL), lambda i: (_mod_row(i, tm, layer), 0, 0)),
            pl.BlockSpec((1, D_MODEL), const),
            pl.BlockSpec(w_down.shape, const),
            pl.BlockSpec((1, MLA_Q_RANK), const),
            pl.BlockSpec(wuq.shape, const),
            pl.BlockSpec((1, MLA_KV_RANK), const),
            pl.BlockSpec(wk.shape, const),
            pl.BlockSpec(wv.shape, const),
            pl.BlockSpec((LANES, tm), const),
            pl.BlockSpec((1, LANES), const),
        ] + _table_specs(tm),
        out_specs=(
            pl.BlockSpec((hq, tm), col),
            pl.BlockSpec((tm, hq), row),
            pl.BlockSpec((hv, tm), col),
            cache(MLA_KV_RANK),
            cache(LANES),
        ),
        compiler_params=_cparams("arbitrary"),
        name="mla_proj",
    )(*xs, mod, g, w_down, gq, wuq, gkv, wk, wv, gqn_t, gkn, *qtabs, *ktabs)


def _mla_cache_kernel(ckv_ref, kpe_ref, wk_ref, wv_ref, gkn_ref, k_ref, vt_ref):
    ckv_b = ckv_ref[...].astype(BF16)
    kcat = jnp.concatenate([ckv_b, kpe_ref[...].astype(BF16)], axis=-1)
    epilogue = functools.partial(
        _store_head_blocks, nq=0, nk=MLA_HEADS, qtabs=None, ktabs=None, gq_t=None,
        gk=gkn_ref[...],
        d_real=MLA_QK, half=MLA_ROPE // 2, qt_ref=None, k_ref=k_ref, vt_ref=vt_ref,
        kc_ref=None, vc_ref=None)
    seg = 2 * LANES
    segments = ([(kcat, wk_ref, s * seg) for s in range(MLA_HEADS // 2)]
                + [(ckv_b, wv_ref, s * seg) for s in range(MLA_NV // 2)])
    _run_segments(segments, epilogue)


def _mla_cache_expand(ckv, kpe_blk, wk, wv, gkn):
    rows = ckv.shape[0]
    tm = TM_PROJ
    hq = MLA_HEADS * LANES
    hv = MLA_HEADS * MLA_V
    row = lambda i: (i, 0)
    const = lambda i: (0, 0)
    return pl.pallas_call(
        _mla_cache_kernel,
        out_shape=(jax.ShapeDtypeStruct((rows, hq), BF16), jax.ShapeDtypeStruct((hv, rows), BF16)),
        grid=(rows // tm,),
        in_specs=[
            pl.BlockSpec((tm, MLA_KV_RANK), row),
            pl.BlockSpec((tm, LANES), row),
            pl.BlockSpec(wk.shape, const),
            pl.BlockSpec(wv.shape, const),
            pl.BlockSpec((1, LANES), const),
        ],
        out_specs=(pl.BlockSpec((tm, hq), row), pl.BlockSpec((hv, tm), lambda i: (0, i))),
        compiler_params=_cparams("arbitrary"),
        name="mla_cache_expand",
    )(ckv, kpe_blk, wk, wv, gkn)


def _attend(s, vt, state):
    m_cur = jnp.max(s, axis=0, keepdims=True)
    if state is None:
        return m_cur, _dot(vt, jnp.exp2(s - m_cur).astype(BF16))
    m, acc = state
    m_new = jnp.maximum(m, m_cur)
    alpha = jnp.exp2(m - m_new)
    return m_new, alpha * acc + _dot(vt, jnp.exp2(s - m_new).astype(BF16))


def _attn_kernel(*refs, mode, n_blocks, k_shared, has_cache, n_keys, ck, lam_init):
    refs = list(refs)
    qt_ref, k_ref, vt_ref = refs[:3]
    pos = 3
    if has_cache:
        kc_ref, vct_ref = refs[pos:pos + 2]
        pos += 2
    if mode == "diff":
        lam_ref, gsub_ref = refs[pos:pos + 2]
        pos += 2
        lp = lam_ref[...]
        lam = (jnp.exp(jnp.sum(lp[0:1] * lp[1:2], axis=-1, keepdims=True))
               - jnp.exp(jnp.sum(lp[2:3] * lp[3:4], axis=-1, keepdims=True)) + lam_init)
    o_ref = refs[pos]

    chunks = []
    if has_cache:
        chunks += [(kc_ref, vct_ref, 0, PAST_LEN)]
    chunks += [(k_ref, vt_ref, c * ck, ck) for c in range(n_keys // ck)]

    n_sub = Q_SPLIT if has_cache else 1
    w = qt_ref.shape[1] // n_sub
    streams = []
    for jb in range(n_blocks):
        if k_shared:
            k_sls = [slice((jb // 2) * LANES, (jb // 2 + 1) * LANES)] * 2
            v_sl = k_sls[0]
        else:
            k_sls = [slice((2 * jb + t) * LANES, (2 * jb + t + 1) * LANES) for t in range(2)]
            v_sl = slice(jb * LANES, (jb + 1) * LANES)
        for sub in range(n_sub):
            for t in range(2):
                q_sl = (slice((2 * jb + t) * LANES, (2 * jb + t + 1) * LANES),
                        slice(sub * w, (sub + 1) * w))
                streams.append((q_sl, k_sls[t], v_sl))

    def scores(item):
        x, c = item
        q_sl, k_sl, _ = streams[x]
        kr, _, start, n_ = chunks[c]
        return _dot(kr[start:start + n_, k_sl], qt_ref[q_sl])

    items = [(x, c) for c in range(len(chunks)) for x in range(len(streams))]
    ahead = len(streams) if has_cache else PROMPT_LOOKAHEAD
    sc = {k: scores(items[k]) for k in range(min(ahead, len(items)))}
    st = [None] * len(streams)
    for k, (x, c) in enumerate(items):
        _, vr, start, n_ = chunks[c]
        ones = (lax.broadcasted_iota(jnp.int32, (ONES_ROWS, n_), 0) == 0).astype(BF16)
        vt_ext = jnp.concatenate([vr[streams[x][2], start:start + n_], ones], axis=0)
        st[x] = _attend(sc.pop(k), vt_ext, st[x])
        if k + ahead < len(items):
            sc[k + ahead] = scores(items[k + ahead])

    for jb in range(n_blocks):
        base = jb * n_sub * 2
        cat = lambda t: jnp.concatenate(
            [st[base + 2 * sub + t][1] for sub in range(n_sub)], axis=1)
        acc_a, acc_b = cat(0), cat(1)
        oa = acc_a[:LANES] * (1.0 / acc_a[LANES:LANES + 1])
        ob = acc_b[:LANES] * (1.0 / acc_b[LANES:LANES + 1])
        if mode == "pair":
            row = lax.broadcasted_iota(jnp.int32, oa.shape, 0)
            o = jnp.where(row < LANES // 2, oa, ob).T
        else:
            o = oa - lam * ob
            ms = jnp.mean(o * o, axis=0, keepdims=True)
            o = (o * lax.rsqrt(ms + EPS)).T * (gsub_ref[...] * (1.0 - lam_init))
        o_ref[:, jb * LANES:(jb + 1) * LANES] = o.astype(BF16)


def _attention(qt, k, vt, kc, vct, extra, *, mode, k_shared, lam_init):
    n_blk = D_MODEL // LANES
    extra_specs_1 = [pl.BlockSpec(e.shape, lambda b: (0, 0)) for e in extra]
    extra_specs_3 = [pl.BlockSpec(e.shape, lambda b, j, i: (0, 0)) for e in extra]

    o_p = pl.pallas_call(
        functools.partial(_attn_kernel, mode=mode, n_blocks=n_blk, k_shared=k_shared,
                          has_cache=False, n_keys=SEQ, ck=SEQ, lam_init=lam_init),
        out_shape=jax.ShapeDtypeStruct((T_PROMPT, D_MODEL), BF16),
        grid=(BATCH,),
        in_specs=[
            pl.BlockSpec((qt.shape[0], SEQ), lambda b: (0, b)),
            pl.BlockSpec((SEQ, k.shape[1]), lambda b: (b, 0)),
            pl.BlockSpec((vt.shape[0], SEQ), lambda b: (0, b)),
        ] + extra_specs_1,
        out_specs=pl.BlockSpec((SEQ, D_MODEL), lambda b: (b, 0)),
        compiler_params=_cparams("arbitrary"),
        name="attn_prompt",
    )(qt, k, vt, *extra)

    tq = TQ_SAMPLE
    kv_col = (lambda j: j // 2) if k_shared else (lambda j: j)
    kw = LANES if k_shared else 2 * LANES
    lat = T_PROMPT // DEC_SEQ
    o_s = pl.pallas_call(
        functools.partial(_attn_kernel, mode=mode, n_blocks=1, k_shared=k_shared,
                          has_cache=True, n_keys=DEC_SEQ, ck=CK, lam_init=lam_init),
        out_shape=jax.ShapeDtypeStruct((T_SAMPLE, D_MODEL), BF16),
        grid=(DEC_BATCH, n_blk, DEC_SEQ // tq),
        in_specs=[
            pl.BlockSpec((2 * LANES, tq),
                         lambda b, j, i: (j, T_PROMPT // tq + b * (DEC_SEQ // tq) + i)),
            pl.BlockSpec((DEC_SEQ, kw), lambda b, j, i: (lat + b, kv_col(j))),
            pl.BlockSpec((LANES, DEC_SEQ), lambda b, j, i: (kv_col(j), lat + b)),
            pl.BlockSpec((PAST_LEN, kw), lambda b, j, i: (b, kv_col(j))),
            pl.BlockSpec((LANES, PAST_LEN), lambda b, j, i: (kv_col(j), b)),
        ] + extra_specs_3,
        out_specs=pl.BlockSpec((tq, LANES), lambda b, j, i: (b * (DEC_SEQ // tq) + i, j)),
        compiler_params=_cparams("arbitrary", "arbitrary", "arbitrary"),
        name="attn_latent",
    )(qt, k, vt, kc, vct, *extra)
    return o_p, o_s


def _route(sel_t, sc_t, tm):
    sel = [sel_t[e:e + 1, :] for e in range(MOE_EXPERTS)]
    sc = [sc_t[e:e + 1, :] for e in range(MOE_EXPERTS)]
    gscore = []
    for g in range(MOE_GROUPS):
        a, b, c, d = sel[4 * g:4 * g + 4]
        hi1, lo1 = jnp.maximum(a, b), jnp.minimum(a, b)
        hi2, lo2 = jnp.maximum(c, d), jnp.minimum(c, d)
        top1 = jnp.maximum(hi1, hi2)
        top2 = jnp.maximum(jnp.minimum(hi1, hi2), jnp.maximum(lo1, lo2))
        gscore.append(top1 + top2)
    gmax = jnp.maximum(jnp.maximum(gscore[0], gscore[1]), jnp.maximum(gscore[2], gscore[3]))
    taken = jnp.zeros_like(gmax)
    gsel = []
    for g in range(MOE_GROUPS):
        hit = jnp.where(gscore[g] == gmax, 1.0, 0.0) * (1.0 - taken)
        gsel.append(hit)
        taken = taken + hit
    vs, ss = [], []
    for e in range(EXPERTS_PER_GROUP):
        vs.append(sum(gsel[g] * sel[4 * g + e] for g in range(MOE_GROUPS)))
        ss.append(sum(gsel[g] * sc[4 * g + e] for g in range(MOE_GROUPS)))
    ws = []
    for i in range(EXPERTS_PER_GROUP):
        beaten = jnp.zeros_like(gmax)
        for j in range(EXPERTS_PER_GROUP):
            if j < i:
                beaten = beaten + jnp.where(vs[j] >= vs[i], 1.0, 0.0)
            elif j > i:
                beaten = beaten + jnp.where(vs[j] > vs[i], 1.0, 0.0)
        ws.append(jnp.where(beaten < 2.0, ss[i], 0.0))
    denom = (ws[0] + ws[1]) + (ws[2] + ws[3])
    inv = 1.0 / denom
    n_rows = 2 * MOE_EXPERTS
    row = lax.broadcasted_iota(jnp.int32, (n_rows, tm), 0)
    comb = jnp.zeros((n_rows, tm), F32)
    for g in range(MOE_GROUPS):
        comb = jnp.where(row == MOE_EXPERTS + g, jnp.broadcast_to(gsel[g], (n_rows, tm)), comb)
        for e in range(EXPERTS_PER_GROUP):
            w = gsel[g] * ws[e] * inv
            comb = jnp.where(row == 4 * g + e, jnp.broadcast_to(w, (n_rows, tm)), comb)
    return comb


def _oproj_kernel(*refs, tm):
    op_ref, os_ref = refs[:2]
    mod_ref, wo_ref, g_ref, wr_ref, br_ref, xo_ref, h_ref, comb_ref = refs[-8:]
    _, _, gt1, sh2, sc2, _ = _mod_slices(mod_ref[0])
    o = _read_rows((op_ref, os_ref), tm)
    x = _read_rows(refs[2:-8], tm)
    wr = wr_ref[...]
    hm = TM_PROJ
    halves = [slice(r0, r0 + hm) for r0 in range(0, tm, hm)]
    attn = [_dot(o[r], wo_ref[...]) for r in halves]
    logits = []
    for r, a in zip(halves, attn):
        xn = x[r] + gt1 * a
        xo_ref[r, :] = xn
        h = _norm_mod(xn, g_ref[...], sc2, sh2)
        hi = h.astype(BF16)
        lo = (h - hi.astype(F32)).astype(BF16)
        h_ref[r, :] = hi
        r1 = _dot(hi, wr)
        logits.append(r1[:, :LANES] + r1[:, LANES:] + _dot(lo, wr[:, :LANES]))
    for r, lg in zip(halves, logits):
        scores = 1.0 / (1.0 + jnp.exp(-lg))
        sel = scores + br_ref[...]
        comb = _route(sel.T, scores.T, hm)
        comb = jnp.concatenate([comb, jnp.zeros((LANES - 2 * MOE_EXPERTS, hm), F32)], axis=0)
        comb_ref[r, :] = comb.T


def _oproj(o_p, o_s, x, mod, wo, g_ffn, wr, br, layer):
    tm = TM_OPROJ
    row = lambda i: (i, 0)
    const = lambda i: (0, 0)
    xs = x if isinstance(x, tuple) else (x,)
    return pl.pallas_call(
        functools.partial(_oproj_kernel, tm=tm),
        out_shape=(
            jax.ShapeDtypeStruct((T_ALL, D_MODEL), F32),
            jax.ShapeDtypeStruct((T_ALL, D_MODEL), BF16),
            jax.ShapeDtypeStruct((T_ALL, LANES), F32),
        ),
        grid=(T_ALL // tm,),
        in_specs=_row_specs((o_p, o_s), tm) + _row_specs(x, tm) + [
            pl.BlockSpec((1, 1, 6 * D_MODEL), lambda i: (_mod_row(i, tm, layer), 0, 0)),
            pl.BlockSpec((D_MODEL, D_MODEL), const),
            pl.BlockSpec((1, D_MODEL), const),
            pl.BlockSpec((D_MODEL, 2 * LANES), const),
            pl.BlockSpec((1, LANES), const),
        ],
        out_specs=(
            pl.BlockSpec((tm, D_MODEL), row),
            pl.BlockSpec((tm, D_MODEL), row),
            pl.BlockSpec((tm, LANES), row),
        ),
        compiler_params=_cparams("arbitrary"),
        name="oproj_router",
    )(o_p, o_s, *xs, mod, wo, g_ffn, wr, br)


def _moe_kernel(h_ref, x_ref, mod_ref, comb_ref, tri_ref, w1_ref, w3_ref, w2_ref, o_ref, acc_ref):
    gt2 = _mod_slices(mod_ref[0])[5]
    comb = comb_ref[...]
    comb_hi = comb.astype(BF16)
    comb_lo = (comb - comb_hi.astype(F32)).astype(BF16)
    rank = _dot(tri_ref[...], comb_hi)
    rank_t, comb_t = rank.T, comb.T
    sub_r = lax.broadcasted_iota(jnp.int32, (MOE_SUB, 1), 0).astype(F32)
    sub_c = lax.broadcasted_iota(jnp.int32, (1, MOE_SUB), 1).astype(F32)

    slots, counts = [], []
    for g in range(MOE_GROUPS):
        lane = MOE_EXPERTS + g
        member_r = comb_t[lane:lane + 1, :]
        member_c = comb[:, lane:lane + 1]
        slots.append((jnp.where(member_r > 0.5, rank_t[lane:lane + 1, :], -1.0),
                      jnp.where(member_c > 0.5, rank[:, lane:lane + 1], -1.0)))
        counts.append(jnp.sum(member_r).astype(jnp.int32))

    def select(g, base):
        slot_r, slot_c = slots[g]
        sel = jnp.where(slot_r - base == sub_r, 1.0, 0.0).astype(BF16)
        sel_t = jnp.where(slot_c - base == sub_c, 1.0, 0.0).astype(BF16)
        return sel, sel_t

    def gather(sel):
        return _dot(sel, h_ref[...]).astype(BF16), _dot(sel, comb_hi) + _dot(sel, comb_lo)

    def experts(g, hc, wc):
        parts = []
        for e in range(EXPERTS_PER_GROUP):
            ex = EXPERTS_PER_GROUP * g + e
            a = _dot(hc, w1_ref[ex])
            u = _dot(hc, w3_ref[ex])
            parts.append((_silu(a) * u * wc[:, ex:ex + 1]).astype(BF16))
        return jnp.concatenate(parts, axis=-1)

    sels = [select(g, 0.0) for g in range(MOE_GROUPS)]
    packed = [gather(sel) for sel, _ in sels]
    hids = [experts(g, *packed[g]) for g in range(MOE_GROUPS)]
    outs = [_dot(hids[g], w2_ref[g]).astype(BF16) for g in range(MOE_GROUPS)]
    acc = _dot(sels[0][1], outs[0])
    for g in range(1, MOE_GROUPS):
        acc = acc + _dot(sels[g][1], outs[g])
    acc_ref[...] = acc

    for g in range(MOE_GROUPS):
        def body(b, carry):
            sel, sel_t = select(g, (b * MOE_SUB).astype(F32))
            hc, wc = gather(sel)
            og = _dot(experts(g, hc, wc), w2_ref[g]).astype(BF16)
            acc_ref[...] += _dot(sel_t, og)
            return carry

        lax.fori_loop(1, lax.div(counts[g] + (MOE_SUB - 1), MOE_SUB), body, 0)
    o_ref[...] = x_ref[...] + gt2 * acc_ref[...]


def _moe(h, x, mod, comb, tri, w1, w3, w2, layer, row0=0, n_rows=T_ALL):
    tm = TM_MOE
    t0 = row0 // tm
    row = lambda i: (t0 + i, 0)
    whole = lambda shape: pl.BlockSpec(shape, lambda i: (0, 0, 0), pipeline_mode=pl.Buffered(1))
    return pl.pallas_call(
        _moe_kernel,
        out_shape=jax.ShapeDtypeStruct((n_rows, D_MODEL), F32),
        grid=(n_rows // tm,),
        in_specs=[
            pl.BlockSpec((tm, D_MODEL), row),
            pl.BlockSpec((tm, D_MODEL), row),
            pl.BlockSpec((1, 1, 6 * D_MODEL), lambda i: (_mod_row(t0 + i, tm, layer), 0, 0)),
            pl.BlockSpec((tm, LANES), row),
            pl.BlockSpec((tm, tm), lambda i: (0, 0)),
            whole(w1.shape), whole(w3.shape), whole(w2.shape),
        ],
        out_specs=pl.BlockSpec((tm, D_MODEL), lambda i: (i, 0)),
        scratch_shapes=[pltpu.VMEM((tm, D_MODEL), F32)],
        compiler_params=_cparams("arbitrary"),
        name="moe",
    )(h, x, mod, comb, tri, w1, w3, w2)


def _rope_tables(d_rot, lane0):
    n = DEC_SEQ
    half = d_rot // 2
    n_freq = d_rot // 4
    t = jnp.arange(n)
    rowp = (t // GRID_W).astype(F32)
    colp = (t % GRID_W).astype(F32)
    inv = jnp.power(ROPE_BASE, -jnp.arange(n_freq, dtype=F32) / n_freq)
    ang = jnp.concatenate([rowp[:, None] * inv, colp[:, None] * inv], axis=-1)
    cos, sin = jnp.cos(ang), jnp.sin(ang)
    c = jnp.ones((n, LANES), F32).at[:, lane0:lane0 + d_rot].set(jnp.concatenate([cos, cos], -1))
    sa = jnp.zeros((n, LANES), F32).at[:, lane0:lane0 + half].set(-sin)
    sb = jnp.zeros((n, LANES), F32).at[:, lane0 + half:lane0 + d_rot].set(sin)
    ident = (jnp.ones((T_PROMPT, LANES), F32), jnp.zeros((T_PROMPT, LANES), F32),
             jnp.zeros((T_PROMPT, LANES), F32))
    tabs = tuple(jnp.concatenate([i_, t_], axis=0) for i_, t_ in zip(ident, (c, sa, sb)))
    return tuple(t_.T for t_ in tabs), tabs


def _gain_t(g):
    return jnp.broadcast_to(g.reshape(LANES, 1), (LANES, TM_PROJ))


def _pad_heads(w, n_heads, d):
    k = w.shape[0]
    w = w.reshape(k, n_heads, d)
    return jnp.pad(w, ((0, 0), (0, 0), (0, LANES - d))).reshape(k, n_heads * LANES)


def _pad_vec(g, scale=1.0):
    return jnp.pad(g.astype(F32) * scale, (0, LANES - g.shape[0])).reshape(1, LANES)


def _pad_rows_to_heads(x, n_heads, d):
    r = x.shape[0]
    x = x.reshape(r, n_heads, d)
    return jnp.pad(x, ((0, 0), (0, 0), (0, LANES - d))).reshape(r, n_heads * LANES)


def kernel(x_prompt, x_sample, cache_mla_ckv, cache_mla_kpe, cache_gqa_k, cache_gqa_v, cache_diff_k, cache_diff_v, c, c_ctx, g_mix, g_ffn, w_mod, b_mod, w_router, b_router, w_e1, w_e3, w_e2, mla_w_dq, mla_g_q, mla_w_uq, mla_w_dkv, mla_g_kv, mla_w_ukv, mla_g_qn, mla_g_kn, mla_w_o, gqa_w_qkv, gqa_g_qn, gqa_g_kn, gqa_w_o, diff_w_qkv, diff_g_qn, diff_g_kn, diff_lam_q1, diff_lam_k1, diff_lam_q2, diff_lam_k2, diff_g_sub, diff_w_o):
    d = D_MODEL
    x = (x_prompt.reshape(T_PROMPT, d), x_sample.reshape(T_SAMPLE, d))

    c_all = jnp.concatenate([c_ctx[None, :], c, jnp.zeros((MOD_ROWS - 1 - DEC_BATCH, d), F32)], axis=0)
    mod = _modulation(c_all, w_mod, b_mod).reshape(DEPTH * MOD_ROWS, 1, 6 * d)

    w_hi = w_router.astype(BF16)
    w_lo = (w_router - w_hi.astype(F32)).astype(BF16)
    pad_r = ((0, 0), (0, LANES - MOE_EXPERTS))
    wr = jnp.concatenate([jnp.pad(w_hi, pad_r), jnp.pad(w_lo, pad_r)], axis=1)
    br = _pad_vec(b_router)

    w1b, w3b, w2b = w_e1.astype(BF16), w_e3.astype(BF16), w_e2.astype(BF16)
    t_idx = jnp.arange(TM_MOE)
    tri = (t_idx[None, :] < t_idx[:, None]).astype(BF16)

    tab_mla = _rope_tables(MLA_ROPE, MLA_NOPE)
    tab_64 = _rope_tables(GQA_HD, 0)

    new_mla, new_gqa, new_diff = [], [], []
    for i in range(DEPTH):
        kind, j = i % 3, i // 3
        g_mix_i = g_mix[i].reshape(1, d)
        if kind == 0:
            w_down = jnp.concatenate(
                [mla_w_dq[j], mla_w_dkv[j],
                 jnp.zeros((d, LANES - MLA_ROPE), F32)], axis=1).astype(BF16)
            wuq = _pad_heads(mla_w_uq[j], MLA_HEADS, MLA_QK).astype(BF16)
            ukv = mla_w_ukv[j].reshape(MLA_KV_RANK, MLA_HEADS, MLA_NOPE + MLA_V)
            wk_nope = jnp.pad(ukv[:, :, :MLA_NOPE], ((0, 0), (0, 0), (0, LANES - MLA_NOPE)))
            place = jnp.zeros((LANES, MLA_HEADS, LANES), F32)
            r = jnp.arange(MLA_ROPE)
            place = place.at[r, :, MLA_NOPE + r].set(1.0)
            wk = jnp.concatenate([wk_nope.reshape(MLA_KV_RANK, -1), place.reshape(LANES, -1)],
                                 axis=0).astype(BF16)
            wv = ukv[:, :, MLA_NOPE:].reshape(MLA_KV_RANK, MLA_HEADS * MLA_V).astype(BF16)
            gqn = _pad_vec(mla_g_qn[j], LOG2E * MLA_QK ** -0.5)
            gkn = _pad_vec(mla_g_kn[j])
            q, k, v, ckv_c, kpe_c = _mla_proj(
                x, mod, g_mix_i, w_down, mla_g_q[j].reshape(1, -1), wuq,
                mla_g_kv[j].reshape(1, -1), wk, wv, _gain_t(gqn), gkn, *tab_mla, i)
            cache_kpe = jnp.pad(cache_mla_kpe[:, j].reshape(DEC_BATCH * PAST_LEN, MLA_ROPE),
                                ((0, 0), (0, LANES - MLA_ROPE)))
            kc, vc = _mla_cache_expand(cache_mla_ckv[:, j].reshape(DEC_BATCH * PAST_LEN, MLA_KV_RANK),
                                       cache_kpe, wk, wv, gkn)
            o_p, o_s = _attention(q, k, v, kc, vc, (), mode="pair", k_shared=False, lam_init=0.0)
            w_o = mla_w_o[j].astype(BF16)
            new_mla.append((ckv_c[:T_PROMPT].reshape(BATCH, SEQ, MLA_KV_RANK),
                            kpe_c[:T_PROMPT, :MLA_ROPE].reshape(BATCH, SEQ, MLA_ROPE)))
        elif kind == 1:
            nq, nk = GQA_Q_HEADS, GQA_KV_HEADS
            wq_, wk_, wv_ = jnp.split(gqa_w_qkv[j], [nq * GQA_HD, (nq + nk) * GQA_HD], axis=1)
            wv_ = wv_.reshape(d, nk, GQA_HD)
            w = jnp.concatenate([_pad_heads(wq_, nq, GQA_HD), _pad_heads(wk_, nk, GQA_HD),
                                 jnp.concatenate([wv_, wv_], axis=-1).reshape(d, nk * LANES)],
                                axis=1).astype(BF16)
            gq, gk = _pad_vec(gqa_g_qn[j], LOG2E * GQA_HD ** -0.5), _pad_vec(gqa_g_kn[j])
            q, k, v, k_c, v_c = _qkv_proj(
                x, mod, g_mix_i, w, _gain_t(gq), gk, *tab_64, i,
                nq=nq, nk=nk, nv=nk * LANES, d_real=GQA_HD, half=GQA_HD // 2)
            rows = DEC_BATCH * PAST_LEN
            kc = _pad_rows_to_heads(cache_gqa_k[:, j].reshape(rows, nk * GQA_HD), nk, GQA_HD).astype(BF16)
            cv = cache_gqa_v[:, j].reshape(rows, nk, GQA_HD)
            vc = jnp.concatenate([cv, cv], axis=-1).reshape(rows, nk * LANES).astype(BF16).T
            o_p, o_s = _attention(q, k, v, kc, vc, (), mode="pair", k_shared=True, lam_init=0.0)
            w_o = gqa_w_o[j].astype(BF16)
            new_gqa.append((k_c[:T_PROMPT].reshape(BATCH, SEQ, nk, LANES)[..., :GQA_HD],
                            v_c[:T_PROMPT].reshape(BATCH, SEQ, nk, LANES)[..., :GQA_HD]))
        else:
            nh = DIFF_HEADS
            lam_init = 0.8 - 0.6 * math.exp(-0.3 * i)
            wq_, wk_, wv_ = jnp.split(diff_w_qkv[j], 3, axis=1)
            w = jnp.concatenate([_pad_heads(wq_, 2 * nh, DIFF_HD), _pad_heads(wk_, 2 * nh, DIFF_HD),
                                 wv_], axis=1).astype(BF16)
            gq, gk = _pad_vec(diff_g_qn[j], LOG2E * DIFF_HD ** -0.5), _pad_vec(diff_g_kn[j])
            q, k, v, k_c, v_c = _qkv_proj(
                x, mod, g_mix_i, w, _gain_t(gq), gk, *tab_64, i,
                nq=2 * nh, nk=2 * nh, nv=nh * 2 * DIFF_HD, d_real=DIFF_HD, half=DIFF_HD // 2)
            rows = DEC_BATCH * PAST_LEN
            kc = _pad_rows_to_heads(cache_diff_k[:, j].reshape(rows, 2 * nh * DIFF_HD), 2 * nh,
                                    DIFF_HD).astype(BF16)
            vc = cache_diff_v[:, j].reshape(rows, nh * 2 * DIFF_HD).astype(BF16).T
            lam_p = jnp.concatenate([_pad_vec(diff_lam_q1[j]), _pad_vec(diff_lam_k1[j]),
                                     _pad_vec(diff_lam_q2[j]), _pad_vec(diff_lam_k2[j])], axis=0)
            o_p, o_s = _attention(q, k, v, kc, vc, (lam_p, diff_g_sub[j].reshape(1, LANES)),
                                  mode="diff", k_shared=False, lam_init=lam_init)
            w_o = diff_w_o[j].astype(BF16)
            new_diff.append((k_c[:T_PROMPT].reshape(BATCH, SEQ, nh, 2, LANES)[..., :DIFF_HD],
                             v_c[:T_PROMPT].reshape(BATCH, SEQ, nh, 2 * DIFF_HD)))

        x, h2, comb = _oproj(o_p, o_s, x, mod, w_o, g_ffn[i].reshape(1, d), wr, br, i)
        w2g = w2b[i].reshape(MOE_GROUPS, EXPERTS_PER_GROUP * MOE_DIM, d)
        if i + 1 < DEPTH:
            x = _moe(h2, x, mod, comb, tri, w1b[i], w3b[i], w2g, i)
        else:
            y_prompt = _moe(h2, x, mod, comb, tri, w1b[i], w3b[i], w2g, i, 0, T_PROMPT)
            y_sample = _moe(h2, x, mod, comb, tri, w1b[i], w3b[i], w2g, i, T_PROMPT, T_SAMPLE)

    y_prompt = y_prompt.reshape(BATCH, SEQ, d)
    y_sample = y_sample.reshape(DEC_BATCH, DEC_SEQ, d)
    stack = lambda items, k: jnp.stack([t[k] for t in items], axis=1)
    return (y_prompt, y_sample, stack(new_mla, 0), stack(new_mla, 1), stack(new_gqa, 0),
            stack(new_gqa, 1), stack(new_diff, 0), stack(new_diff, 1))
```

```python
import functools
import math

import jax
import jax.numpy as jnp
from jax import lax
from jax.experimental import pallas as pl
from jax.experimental.pallas import tpu as pltpu

F32 = jnp.float32
BF16 = jnp.bfloat16

D_MODEL = 1024
BATCH, SEQ = 16, 256
DEC_BATCH, DEC_SEQ = 2, 4096
PAST_LEN = 512
DEPTH = 4
GRID_W = 64
EPS = 1e-6
ROPE_BASE = 10000.0
LOG2E = math.log2(math.e)
MLA_HEADS, MLA_NOPE, MLA_ROPE, MLA_QK, MLA_V = 16, 64, 32, 96, 64
MLA_Q_RANK, MLA_KV_RANK = 384, 256
GQA_Q_HEADS, GQA_KV_HEADS, GQA_HD = 16, 4, 64
DIFF_HEADS, DIFF_HD = 8, 64
MOE_EXPERTS, MOE_GROUPS, EXPERTS_PER_GROUP, MOE_DIM = 16, 4, 4, 256

T_PROMPT = BATCH * SEQ
T_SAMPLE = DEC_BATCH * DEC_SEQ
T_ALL = T_PROMPT + T_SAMPLE

LANES = 128
VMEM_LIMIT_BYTES = 56 * 1024 * 1024

TM_PROJ = 256
TM_OPROJ = 1024
TM_MOE = 512
MOE_SUB = 160
TQ_SAMPLE = 2048
PROMPT_LOOKAHEAD = 8
Q_SPLIT = 8
ONES_ROWS = 16
CK = 512
MOD_NT = 1536
MOD_ROWS = 8


def _cparams(*sem):
    return pltpu.CompilerParams(dimension_semantics=sem, vmem_limit_bytes=VMEM_LIMIT_BYTES)


def _dot(a, b):
    return jnp.dot(a, b, preferred_element_type=F32)


def _rms(x, g):
    ms = jnp.mean(x * x, axis=-1, keepdims=True)
    return x * lax.rsqrt(ms + EPS) * g


def _norm_mod(x, g, scale, shift):
    return _rms(x, g) * (1.0 + scale) + shift


def _silu(x):
    return x / (1.0 + jnp.exp(-x))


def _mod_slices(m):
    d = D_MODEL
    return [m[:, k * d:(k + 1) * d] for k in range(6)]


def _mod_kernel(c_ref, w_ref, b_ref, o_ref):
    s = _silu(c_ref[...]).astype(BF16)
    o_ref[0] = _dot(s, w_ref[0].astype(BF16)) + b_ref[0]


def _modulation(c_all, w_mod, b_mod):
    n = 6 * D_MODEL
    return pl.pallas_call(
        _mod_kernel,
        out_shape=jax.ShapeDtypeStruct((DEPTH, MOD_ROWS, n), F32),
        grid=(DEPTH, n // MOD_NT),
        in_specs=[
            pl.BlockSpec((MOD_ROWS, D_MODEL), lambda l, j: (0, 0)),
            pl.BlockSpec((1, D_MODEL, MOD_NT), lambda l, j: (l, 0, j)),
            pl.BlockSpec((1, 1, MOD_NT), lambda l, j: (l, 0, j)),
        ],
        out_specs=pl.BlockSpec((1, MOD_ROWS, MOD_NT), lambda l, j: (l, 0, j)),
        compiler_params=_cparams("arbitrary", "arbitrary"),
        name="modulation",
    )(c_all, w_mod, b_mod.reshape(DEPTH, 1, n))


def _mod_row(i, tm, layer):
    r = jnp.where(i * tm < T_PROMPT, 0, 1 + (i * tm - T_PROMPT) // DEC_SEQ)
    return layer * MOD_ROWS + r


def _table_blk(i, tm):
    npt = T_PROMPT // tm
    return jnp.where(i < npt, i, npt + (i - npt) % (DEC_SEQ // tm))


def _row_specs(x, tm):
    if not isinstance(x, tuple):
        return [pl.BlockSpec((tm, D_MODEL), lambda i: (i, 0))]
    npt = T_PROMPT // tm
    return [pl.BlockSpec((tm, D_MODEL), lambda i: (jnp.minimum(i, npt - 1), 0)),
            pl.BlockSpec((tm, D_MODEL), lambda i: (jnp.maximum(i - npt, 0), 0))]


def _read_rows(refs, tm):
    if len(refs) == 1:
        return refs[0][...]
    return jnp.where(pl.program_id(0) < T_PROMPT // tm, refs[0][...], refs[1][...])


def _cache_blk(i, tm):
    return jnp.minimum(i, T_PROMPT // tm)


def _q_epilogue(y, g_t, tabs, d_real, half, with_plain=False):
    yt = y.T
    ms = jnp.sum(yt * yt, axis=0, keepdims=True) * (1.0 / d_real)
    qn = yt * lax.rsqrt(ms + EPS) * g_t
    if tabs is None:
        return qn
    ct, sat, sbt = tabs
    up = jnp.concatenate([qn[half:], qn[:half]], axis=0)
    dn = jnp.concatenate([qn[LANES - half:], qn[:LANES - half]], axis=0)
    rot = qn * ct + up * sat + dn * sbt
    return (qn, rot) if with_plain else rot


def _run_segments(segments, epilogue):
    prev = None
    for idx, (lhs, w_ref, col) in enumerate(segments):
        y = _dot(lhs, w_ref[:, col:col + 2 * LANES])
        if prev is not None:
            epilogue(*prev)
        prev = (idx, y)
    epilogue(*prev)


def _store_head_blocks(idx, y, nq, nk, qtabs, gq_t, gk_t, d_real, half,
                       qt_ref, k_ref, vt_ref, kc_ref, vc_ref):
    j = 2 * idx
    blks = [y[:, :LANES], y[:, LANES:]]
    if j < nq:
        for hb in range(2):
            qt_ref[(j + hb) * LANES:(j + hb + 1) * LANES, :] = _q_epilogue(
                blks[hb], gq_t, qtabs, d_real, half).astype(BF16)
    elif j < nq + nk:
        j -= nq
        for hb in range(2):
            sl = slice((j + hb) * LANES, (j + hb + 1) * LANES)
            if kc_ref is None:
                k_ref[:, sl] = _q_epilogue(blks[hb], gk_t, qtabs, d_real, half).T.astype(BF16)
            else:
                kc, kr = _q_epilogue(blks[hb], gk_t, qtabs, d_real, half, with_plain=True)
                kc_ref[:, sl] = kc.T
                k_ref[:, sl] = kr.T.astype(BF16)
    else:
        j -= nq + nk
        if vc_ref is not None:
            vc_ref[:, j * LANES:(j + 2) * LANES] = y
        for hb in range(2):
            vt_ref[(j + hb) * LANES:(j + hb + 1) * LANES, :] = blks[hb].T.astype(BF16)


def _qkv_kernel(x_ref, mod_ref, g_ref, w_ref, gq_ref, gk_ref, qc_ref, qsa_ref, qsb_ref,
                qt_ref, k_ref, vt_ref, kc_ref, vc_ref, *, nq, nk, d_real, half):
    sh, sc = _mod_slices(mod_ref[0])[:2]
    h = _norm_mod(x_ref[...], g_ref[...], sc, sh).astype(BF16)
    qtabs = (qc_ref[...], qsa_ref[...], qsb_ref[...])
    epilogue = functools.partial(
        _store_head_blocks, nq=nq, nk=nk, qtabs=qtabs, gq_t=gq_ref[...], gk_t=gk_ref[...],
        d_real=d_real,
        half=half, qt_ref=qt_ref, k_ref=k_ref, vt_ref=vt_ref, kc_ref=kc_ref, vc_ref=vc_ref)
    n_seg = w_ref.shape[1] // (2 * LANES)
    _run_segments([(h, w_ref, s * 2 * LANES) for s in range(n_seg)], epilogue)


def _table_specs(tm):
    tab = pl.BlockSpec((LANES, tm), lambda i: (0, _table_blk(i, tm)))
    return [tab, tab, tab]


def _qkv_proj(x, mod, g, w, gq_t, gk_t, tabs, layer, *, nq, nk, nv, d_real, half):
    tm = TM_PROJ
    n = w.shape[1]
    wq, wk = nq * LANES, nk * LANES
    row = lambda i: (i, 0)
    col = lambda i: (0, i)
    const = lambda i: (0, 0)
    cache = lambda width: pl.BlockSpec((tm, width), lambda i: (_cache_blk(i, tm), 0))
    return pl.pallas_call(
        functools.partial(_qkv_kernel, nq=nq, nk=nk, d_real=d_real, half=half),
        out_shape=(
            jax.ShapeDtypeStruct((wq, T_ALL), BF16),
            jax.ShapeDtypeStruct((T_ALL, wk), BF16),
            jax.ShapeDtypeStruct((nv, T_ALL), BF16),
            jax.ShapeDtypeStruct((T_PROMPT + tm, wk), F32),
            jax.ShapeDtypeStruct((T_PROMPT + tm, nv), F32),
        ),
        grid=(T_ALL // tm,),
        in_specs=[
            pl.BlockSpec((tm, D_MODEL), row),
            pl.BlockSpec((1, 1, 6 * D_MODEL), lambda i: (_mod_row(i, tm, layer), 0, 0)),
            pl.BlockSpec((1, D_MODEL), const),
            pl.BlockSpec((D_MODEL, n), const),
            pl.BlockSpec((LANES, tm), const),
            pl.BlockSpec((LANES, tm), const),
        ] + _table_specs(tm),
        out_specs=(
            pl.BlockSpec((wq, tm), col),
            pl.BlockSpec((tm, wk), row),
            pl.BlockSpec((nv, tm), col),
            cache(wk),
            cache(nv),
        ),
        compiler_params=_cparams("arbitrary"),
        name="qkv_proj",
    )(x, mod, g, w, gq_t, gk_t, *tabs)


MLA_NQ = MLA_HEADS
MLA_NV = MLA_HEADS * MLA_V // LANES


def _mla_kernel(*refs, tm):
    (mod_ref, g_ref, wd_ref, gq_ref, wuq_ref, gkv_ref, wk_ref, wv_ref, gqn_ref, gkn_ref,
     qc_ref, qsa_ref, qsb_ref, qt_ref, k_ref, vt_ref, ckv_ref, kpe_ref) = refs[-18:]
    sh, sc = _mod_slices(mod_ref[0])[:2]
    h = _norm_mod(_read_rows(refs[:-18], tm), g_ref[...], sc, sh).astype(BF16)
    y = _dot(h, wd_ref[...])
    cqn = _rms(y[:, :MLA_Q_RANK], gq_ref[...]).astype(BF16)
    ckvn = _rms(y[:, MLA_Q_RANK:MLA_Q_RANK + MLA_KV_RANK], gkv_ref[...])
    kpe_blk = y[:, MLA_Q_RANK + MLA_KV_RANK:]
    ckv_ref[...] = ckvn
    kpe_ref[...] = kpe_blk
    ckv_b = ckvn.astype(BF16)
    kcat = jnp.concatenate([ckv_b, kpe_blk.astype(BF16)], axis=-1)
    qtabs = (qc_ref[...], qsa_ref[...], qsb_ref[...])
    epilogue = functools.partial(
        _store_head_blocks, nq=MLA_NQ, nk=MLA_HEADS, qtabs=qtabs,
        gq_t=gqn_ref[...], gk_t=gkn_ref[...],
        d_real=MLA_QK, half=MLA_ROPE // 2, qt_ref=qt_ref, k_ref=k_ref, vt_ref=vt_ref,
        kc_ref=None, vc_ref=None)
    seg = 2 * LANES
    segments = ([(cqn, wuq_ref, s * seg) for s in range(MLA_NQ // 2)]
                + [(kcat, wk_ref, s * seg) for s in range(MLA_HEADS // 2)]
                + [(ckv_b, wv_ref, s * seg) for s in range(MLA_NV // 2)])
    _run_segments(segments, epilogue)


def _mla_proj(x, mod, g, w_down, gq, wuq, gkv, wk, wv, gqn_t, gkn_t, tabs, layer):
    tm = TM_PROJ
    xs = x if isinstance(x, tuple) else (x,)
    hq = MLA_HEADS * LANES
    hv = MLA_HEADS * MLA_V
    row = lambda i: (i, 0)
    col = lambda i: (0, i)
    const = lambda i: (0, 0)
    cache = lambda width: pl.BlockSpec((tm, width), lambda i: (_cache_blk(i, tm), 0))
    return pl.pallas_call(
        functools.partial(_mla_kernel, tm=tm),
        out_shape=(
            jax.ShapeDtypeStruct((hq, T_ALL), BF16),
            jax.ShapeDtypeStruct((T_ALL, hq), BF16),
            jax.ShapeDtypeStruct((hv, T_ALL), BF16),
            jax.ShapeDtypeStruct((T_PROMPT + tm, MLA_KV_RANK), F32),
            jax.ShapeDtypeStruct((T_PROMPT + tm, LANES), F32),
        ),
        grid=(T_ALL // tm,),
        in_specs=_row_specs(x, tm) + [
            pl.BlockSpec((1, 1, 6 * D_MODEL), lambda i: (_mod_row(i, tm, layer), 0, 0)),
            pl.BlockSpec((1, D_MODEL), const),
            pl.BlockSpec(w_down.shape, const),
            pl.BlockSpec((1, MLA_Q_RANK), const),
            pl.BlockSpec(wuq.shape, const),
            pl.BlockSpec((1, MLA_KV_RANK), const),
            pl.BlockSpec(wk.shape, const),
            pl.BlockSpec(wv.shape, const),
            pl.BlockSpec((LANES, tm), const),
            pl.BlockSpec((LANES, tm), const),
        ] + _table_specs(tm),
        out_specs=(
            pl.BlockSpec((hq, tm), col),
            pl.BlockSpec((tm, hq), row),
            pl.BlockSpec((hv, tm), col),
            cache(MLA_KV_RANK),
            cache(LANES),
        ),
        compiler_params=_cparams("arbitrary"),
        name="mla_proj",
    )(*xs, mod, g, w_down, gq, wuq, gkv, wk, wv, gqn_t, gkn_t, *tabs)


def _mla_cache_kernel(ckv_ref, kpe_ref, wk_ref, wv_ref, gkn_ref, k_ref, vt_ref):
    ckv_b = ckv_ref[...].astype(BF16)
    kcat = jnp.concatenate([ckv_b, kpe_ref[...].astype(BF16)], axis=-1)
    epilogue = functools.partial(
        _store_head_blocks, nq=0, nk=MLA_HEADS, qtabs=None, gq_t=None, gk_t=gkn_ref[...],
        d_real=MLA_QK, half=MLA_ROPE // 2, qt_ref=None, k_ref=k_ref, vt_ref=vt_ref,
        kc_ref=None, vc_ref=None)
    seg = 2 * LANES
    segments = ([(kcat, wk_ref, s * seg) for s in range(MLA_HEADS // 2)]
                + [(ckv_b, wv_ref, s * seg) for s in range(MLA_NV // 2)])
    _run_segments(segments, epilogue)


def _mla_cache_expand(ckv, kpe_blk, wk, wv, gkn):
    rows = ckv.shape[0]
    tm = TM_PROJ
    hq = MLA_HEADS * LANES
    hv = MLA_HEADS * MLA_V
    row = lambda i: (i, 0)
    const = lambda i: (0, 0)
    return pl.pallas_call(
        _mla_cache_kernel,
        out_shape=(jax.ShapeDtypeStruct((rows, hq), BF16), jax.ShapeDtypeStruct((hv, rows), BF16)),
        grid=(rows // tm,),
        in_specs=[
            pl.BlockSpec((tm, MLA_KV_RANK), row),
            pl.BlockSpec((tm, LANES), row),
            pl.BlockSpec(wk.shape, const),
            pl.BlockSpec(wv.shape, const),
            pl.BlockSpec((LANES, tm), const),
        ],
        out_specs=(pl.BlockSpec((tm, hq), row), pl.BlockSpec((hv, tm), lambda i: (0, i))),
        compiler_params=_cparams("arbitrary"),
        name="mla_cache_expand",
    )(ckv, kpe_blk, wk, wv, gkn)


def _attend(s, vt, state):
    m_cur = jnp.max(s, axis=0, keepdims=True)
    if state is None:
        return m_cur, _dot(vt, jnp.exp2(s - m_cur).astype(BF16))
    m, acc = state
    m_new = jnp.maximum(m, m_cur)
    alpha = jnp.exp2(m - m_new)
    return m_new, alpha * acc + _dot(vt, jnp.exp2(s - m_new).astype(BF16))


def _attn_kernel(*refs, mode, n_blocks, k_shared, has_cache, n_keys, ck, lam_init):
    refs = list(refs)
    qt_ref, k_ref, vt_ref = refs[:3]
    pos = 3
    if has_cache:
        kc_ref, vct_ref = refs[pos:pos + 2]
        pos += 2
    if mode == "diff":
        lam_ref, gsub_ref = refs[pos:pos + 2]
        pos += 2
        lp = lam_ref[...]
        lam = (jnp.exp(jnp.sum(lp[0:1] * lp[1:2], axis=-1, keepdims=True))
               - jnp.exp(jnp.sum(lp[2:3] * lp[3:4], axis=-1, keepdims=True)) + lam_init)
    o_ref = refs[pos]

    chunks = []
    if has_cache:
        chunks += [(kc_ref, vct_ref, 0, PAST_LEN)]
    chunks += [(k_ref, vt_ref, c * ck, ck) for c in range(n_keys // ck)]

    n_sub = Q_SPLIT if has_cache else 1
    w = qt_ref.shape[1] // n_sub
    streams = []
    for jb in range(n_blocks):
        if k_shared:
            k_sls = [slice((jb // 2) * LANES, (jb // 2 + 1) * LANES)] * 2
            v_sl = k_sls[0]
        else:
            k_sls = [slice((2 * jb + t) * LANES, (2 * jb + t + 1) * LANES) for t in range(2)]
            v_sl = slice(jb * LANES, (jb + 1) * LANES)
        for sub in range(n_sub):
            for t in range(2):
                q_sl = (slice((2 * jb + t) * LANES, (2 * jb + t + 1) * LANES),
                        slice(sub * w, (sub + 1) * w))
                streams.append((q_sl, k_sls[t], v_sl))

    def scores(item):
        x, c = item
        q_sl, k_sl, _ = streams[x]
        kr, _, start, n_ = chunks[c]
        return _dot(kr[start:start + n_, k_sl], qt_ref[q_sl])

    items = [(x, c) for c in range(len(chunks)) for x in range(len(streams))]
    ahead = len(streams) if has_cache else PROMPT_LOOKAHEAD
    sc = {k: scores(items[k]) for k in range(min(ahead, len(items)))}
    st = [None] * len(streams)
    for k, (x, c) in enumerate(items):
        _, vr, start, n_ = chunks[c]
        ones = (lax.broadcasted_iota(jnp.int32, (ONES_ROWS, n_), 0) == 0).astype(BF16)
        vt_ext = jnp.concatenate([vr[streams[x][2], start:start + n_], ones], axis=0)
        st[x] = _attend(sc.pop(k), vt_ext, st[x])
        if k + ahead < len(items):
            sc[k + ahead] = scores(items[k + ahead])

    for jb in range(n_blocks):
        base = jb * n_sub * 2
        cat = lambda t: jnp.concatenate(
            [st[base + 2 * sub + t][1] for sub in range(n_sub)], axis=1)
        acc_a, acc_b = cat(0), cat(1)
        oa = acc_a[:LANES] * (1.0 / acc_a[LANES:LANES + 1])
        ob = acc_b[:LANES] * (1.0 / acc_b[LANES:LANES + 1])
        if mode == "pair":
            row = lax.broadcasted_iota(jnp.int32, oa.shape, 0)
            o = jnp.where(row < LANES // 2, oa, ob).T
        else:
            o = oa - lam * ob
            ms = jnp.mean(o * o, axis=0, keepdims=True)
            o = (o * lax.rsqrt(ms + EPS)).T * (gsub_ref[...] * (1.0 - lam_init))
        o_ref[:, jb * LANES:(jb + 1) * LANES] = o.astype(BF16)


def _attention(qt, k, vt, kc, vct, extra, *, mode, k_shared, lam_init):
    n_blk = D_MODEL // LANES
    extra_specs_1 = [pl.BlockSpec(e.shape, lambda b: (0, 0)) for e in extra]
    extra_specs_3 = [pl.BlockSpec(e.shape, lambda b, j, i: (0, 0)) for e in extra]

    o_p = pl.pallas_call(
        functools.partial(_attn_kernel, mode=mode, n_blocks=n_blk, k_shared=k_shared,
                          has_cache=False, n_keys=SEQ, ck=SEQ, lam_init=lam_init),
        out_shape=jax.ShapeDtypeStruct((T_PROMPT, D_MODEL), BF16),
        grid=(BATCH,),
        in_specs=[
            pl.BlockSpec((qt.shape[0], SEQ), lambda b: (0, b)),
            pl.BlockSpec((SEQ, k.shape[1]), lambda b: (b, 0)),
            pl.BlockSpec((vt.shape[0], SEQ), lambda b: (0, b)),
        ] + extra_specs_1,
        out_specs=pl.BlockSpec((SEQ, D_MODEL), lambda b: (b, 0)),
        compiler_params=_cparams("arbitrary"),
        name="attn_prompt",
    )(qt, k, vt, *extra)

    tq = TQ_SAMPLE
    kv_col = (lambda j: j // 2) if k_shared else (lambda j: j)
    kw = LANES if k_shared else 2 * LANES
    lat = T_PROMPT // DEC_SEQ
    o_s = pl.pallas_call(
        functools.partial(_attn_kernel, mode=mode, n_blocks=1, k_shared=k_shared,
                          has_cache=True, n_keys=DEC_SEQ, ck=CK, lam_init=lam_init),
        out_shape=jax.ShapeDtypeStruct((T_SAMPLE, D_MODEL), BF16),
        grid=(DEC_BATCH, n_blk, DEC_SEQ // tq),
        in_specs=[
            pl.BlockSpec((2 * LANES, tq),
                         lambda b, j, i: (j, T_PROMPT // tq + b * (DEC_SEQ // tq) + i)),
            pl.BlockSpec((DEC_SEQ, kw), lambda b, j, i: (lat + b, kv_col(j))),
            pl.BlockSpec((LANES, DEC_SEQ), lambda b, j, i: (kv_col(j), lat + b)),
            pl.BlockSpec((PAST_LEN, kw), lambda b, j, i: (b, kv_col(j))),
            pl.BlockSpec((LANES, PAST_LEN), lambda b, j, i: (kv_col(j), b)),
        ] + extra_specs_3,
        out_specs=pl.BlockSpec((tq, LANES), lambda b, j, i: (b * (DEC_SEQ // tq) + i, j)),
        compiler_params=_cparams("arbitrary", "arbitrary", "arbitrary"),
        name="attn_latent",
    )(qt, k, vt, kc, vct, *extra)
    return o_p, o_s


def _route(sel_t, sc_t, tm):
    sel = [sel_t[e:e + 1, :] for e in range(MOE_EXPERTS)]
    sc = [sc_t[e:e + 1, :] for e in range(MOE_EXPERTS)]
    gscore = []
    for g in range(MOE_GROUPS):
        a, b, c, d = sel[4 * g:4 * g + 4]
        hi1, lo1 = jnp.maximum(a, b), jnp.minimum(a, b)
        hi2, lo2 = jnp.maximum(c, d), jnp.minimum(c, d)
        top1 = jnp.maximum(hi1, hi2)
        top2 = jnp.maximum(jnp.minimum(hi1, hi2), jnp.maximum(lo1, lo2))
        gscore.append(top1 + top2)
    gmax = jnp.maximum(jnp.maximum(gscore[0], gscore[1]), jnp.maximum(gscore[2], gscore[3]))
    taken = jnp.zeros_like(gmax)
    gsel = []
    for g in range(MOE_GROUPS):
        hit = jnp.where(gscore[g] == gmax, 1.0, 0.0) * (1.0 - taken)
        gsel.append(hit)
        taken = taken + hit
    vs, ss = [], []
    for e in range(EXPERTS_PER_GROUP):
        vs.append(sum(gsel[g] * sel[4 * g + e] for g in range(MOE_GROUPS)))
        ss.append(sum(gsel[g] * sc[4 * g + e] for g in range(MOE_GROUPS)))
    ws = []
    for i in range(EXPERTS_PER_GROUP):
        beaten = jnp.zeros_like(gmax)
        for j in range(EXPERTS_PER_GROUP):
            if j < i:
                beaten = beaten + jnp.where(vs[j] >= vs[i], 1.0, 0.0)
            elif j > i:
                beaten = beaten + jnp.where(vs[j] > vs[i], 1.0, 0.0)
        ws.append(jnp.where(beaten < 2.0, ss[i], 0.0))
    denom = (ws[0] + ws[1]) + (ws[2] + ws[3])
    inv = 1.0 / denom
    n_rows = 2 * MOE_EXPERTS
    row = lax.broadcasted_iota(jnp.int32, (n_rows, tm), 0)
    comb = jnp.zeros((n_rows, tm), F32)
    for g in range(MOE_GROUPS):
        comb = jnp.where(row == MOE_EXPERTS + g, jnp.broadcast_to(gsel[g], (n_rows, tm)), comb)
        for e in range(EXPERTS_PER_GROUP):
            w = gsel[g] * ws[e] * inv
            comb = jnp.where(row == 4 * g + e, jnp.broadcast_to(w, (n_rows, tm)), comb)
    return comb


def _oproj_kernel(*refs, tm):
    op_ref, os_ref = refs[:2]
    mod_ref, wo_ref, g_ref, wr_ref, br_ref, xo_ref, h_ref, comb_ref = refs[-8:]
    _, _, gt1, sh2, sc2, _ = _mod_slices(mod_ref[0])
    o = _read_rows((op_ref, os_ref), tm)
    x = _read_rows(refs[2:-8], tm)
    wr = wr_ref[...]
    hm = TM_PROJ
    halves = [slice(r0, r0 + hm) for r0 in range(0, tm, hm)]
    attn = [_dot(o[r], wo_ref[...]) for r in halves]
    logits = []
    for r, a in zip(halves, attn):
        xn = x[r] + gt1 * a
        xo_ref[r, :] = xn
        h = _norm_mod(xn, g_ref[...], sc2, sh2)
        hi = h.astype(BF16)
        lo = (h - hi.astype(F32)).astype(BF16)
        h_ref[r, :] = hi
        r1 = _dot(hi, wr)
        logits.append(r1[:, :LANES] + r1[:, LANES:] + _dot(lo, wr[:, :LANES]))
    for r, lg in zip(halves, logits):
        scores = 1.0 / (1.0 + jnp.exp(-lg))
        sel = scores + br_ref[...]
        comb = _route(sel.T, scores.T, hm)
        comb = jnp.concatenate([comb, jnp.zeros((LANES - 2 * MOE_EXPERTS, hm), F32)], axis=0)
        comb_ref[r, :] = comb.T


def _oproj(o_p, o_s, x, mod, wo, g_ffn, wr, br, layer):
    tm = TM_OPROJ
    row = lambda i: (i, 0)
    const = lambda i: (0, 0)
    xs = x if isinstance(x, tuple) else (x,)
    return pl.pallas_call(
        functools.partial(_oproj_kernel, tm=tm),
        out_shape=(
            jax.ShapeDtypeStruct((T_ALL, D_MODEL), F32),
            jax.ShapeDtypeStruct((T_ALL, D_MODEL), BF16),
            jax.ShapeDtypeStruct((T_ALL, LANES), F32),
        ),
        grid=(T_ALL // tm,),
        in_specs=_row_specs((o_p, o_s), tm) + _row_specs(x, tm) + [
            pl.BlockSpec((1, 1, 6 * D_MODEL), lambda i: (_mod_row(i, tm, layer), 0, 0)),
            pl.BlockSpec((D_MODEL, D_MODEL), const),
            pl.BlockSpec((1, D_MODEL), const),
            pl.BlockSpec((D_MODEL, 2 * LANES), const),
            pl.BlockSpec((1, LANES), const),
        ],
        out_specs=(
            pl.BlockSpec((tm, D_MODEL), row),
            pl.BlockSpec((tm, D_MODEL), row),
            pl.BlockSpec((tm, LANES), row),
        ),
        compiler_params=_cparams("arbitrary"),
        name="oproj_router",
    )(o_p, o_s, *xs, mod, wo, g_ffn, wr, br)


def _moe_kernel(h_ref, x_ref, mod_ref, comb_ref, tri_ref, w1_ref, w3_ref, w2_ref, o_ref, acc_ref):
    gt2 = _mod_slices(mod_ref[0])[5]
    comb = comb_ref[...]
    comb_hi = comb.astype(BF16)
    comb_lo = (comb - comb_hi.astype(F32)).astype(BF16)
    rank = _dot(tri_ref[...], comb_hi)
    rank_t, comb_t = rank.T, comb.T
    sub_r = lax.broadcasted_iota(jnp.int32, (MOE_SUB, 1), 0).astype(F32)
    sub_c = lax.broadcasted_iota(jnp.int32, (1, MOE_SUB), 1).astype(F32)

    slots, counts = [], []
    for g in range(MOE_GROUPS):
        lane = MOE_EXPERTS + g
        member_r = comb_t[lane:lane + 1, :]
        member_c = comb[:, lane:lane + 1]
        slots.append((jnp.where(member_r > 0.5, rank_t[lane:lane + 1, :], -1.0),
                      jnp.where(member_c > 0.5, rank[:, lane:lane + 1], -1.0)))
        counts.append(jnp.sum(member_r).astype(jnp.int32))

    def select(g, base):
        slot_r, slot_c = slots[g]
        sel = jnp.where(slot_r - base == sub_r, 1.0, 0.0).astype(BF16)
        sel_t = jnp.where(slot_c - base == sub_c, 1.0, 0.0).astype(BF16)
        return sel, sel_t

    def gather(sel):
        return _dot(sel, h_ref[...]).astype(BF16), _dot(sel, comb_hi) + _dot(sel, comb_lo)

    def experts(g, hc, wc):
        parts = []
        for e in range(EXPERTS_PER_GROUP):
            ex = EXPERTS_PER_GROUP * g + e
            a = _dot(hc, w1_ref[ex])
            u = _dot(hc, w3_ref[ex])
            parts.append((_silu(a) * u * wc[:, ex:ex + 1]).astype(BF16))
        return jnp.concatenate(parts, axis=-1)

    sels = [select(g, 0.0) for g in range(MOE_GROUPS)]
    packed = [gather(sel) for sel, _ in sels]
    hids = [experts(g, *packed[g]) for g in range(MOE_GROUPS)]
    outs = [_dot(hids[g], w2_ref[g]).astype(BF16) for g in range(MOE_GROUPS)]
    acc = _dot(sels[0][1], outs[0])
    for g in range(1, MOE_GROUPS):
        acc = acc + _dot(sels[g][1], outs[g])
    acc_ref[...] = acc

    for g in range(MOE_GROUPS):
        def body(b, carry):
            sel, sel_t = select(g, (b * MOE_SUB).astype(F32))
            hc, wc = gather(sel)
            og = _dot(experts(g, hc, wc), w2_ref[g]).astype(BF16)
            acc_ref[...] += _dot(sel_t, og)
            return carry

        lax.fori_loop(1, lax.div(counts[g] + (MOE_SUB - 1), MOE_SUB), body, 0)
    o_ref[...] = x_ref[...] + gt2 * acc_ref[...]


def _moe(h, x, mod, comb, tri, w1, w3, w2, layer, row0=0, n_rows=T_ALL):
    tm = TM_MOE
    t0 = row0 // tm
    row = lambda i: (t0 + i, 0)
    whole = lambda shape: pl.BlockSpec(shape, lambda i: (0, 0, 0), pipeline_mode=pl.Buffered(1))
    return pl.pallas_call(
        _moe_kernel,
        out_shape=jax.ShapeDtypeStruct((n_rows, D_MODEL), F32),
        grid=(n_rows // tm,),
        in_specs=[
            pl.BlockSpec((tm, D_MODEL), row),
            pl.BlockSpec((tm, D_MODEL), row),
            pl.BlockSpec((1, 1, 6 * D_MODEL), lambda i: (_mod_row(t0 + i, tm, layer), 0, 0)),
            pl.BlockSpec((tm, LANES), row),
            pl.BlockSpec((tm, tm), lambda i: (0, 0)),
            whole(w1.shape), whole(w3.shape), whole(w2.shape),
        ],
        out_specs=pl.BlockSpec((tm, D_MODEL), lambda i: (i, 0)),
        scratch_shapes=[pltpu.VMEM((tm, D_MODEL), F32)],
        compiler_params=_cparams("arbitrary"),
        name="moe",
    )(h, x, mod, comb, tri, w1, w3, w2)


def _rope_tables(d_rot, lane0):
    n = DEC_SEQ
    half = d_rot // 2
    n_freq = d_rot // 4
    t = jnp.arange(n)
    rowp = (t // GRID_W).astype(F32)
    colp = (t % GRID_W).astype(F32)
    inv = jnp.power(ROPE_BASE, -jnp.arange(n_freq, dtype=F32) / n_freq)
    ang = jnp.concatenate([rowp[:, None] * inv, colp[:, None] * inv], axis=-1)
    cos, sin = jnp.cos(ang), jnp.sin(ang)
    c = jnp.ones((n, LANES), F32).at[:, lane0:lane0 + d_rot].set(jnp.concatenate([cos, cos], -1))
    sa = jnp.zeros((n, LANES), F32).at[:, lane0:lane0 + half].set(-sin)
    sb = jnp.zeros((n, LANES), F32).at[:, lane0 + half:lane0 + d_rot].set(sin)
    ident = (jnp.ones((T_PROMPT, LANES), F32), jnp.zeros((T_PROMPT, LANES), F32),
             jnp.zeros((T_PROMPT, LANES), F32))
    return tuple(jnp.concatenate([i_, t_], axis=0).T for i_, t_ in zip(ident, (c, sa, sb)))


def _gain_t(g):
    return jnp.broadcast_to(g.reshape(LANES, 1), (LANES, TM_PROJ))


def _pad_heads(w, n_heads, d):
    k = w.shape[0]
    w = w.reshape(k, n_heads, d)
    return jnp.pad(w, ((0, 0), (0, 0), (0, LANES - d))).reshape(k, n_heads * LANES)


def _pad_vec(g, scale=1.0):
    return jnp.pad(g.astype(F32) * scale, (0, LANES - g.shape[0])).reshape(1, LANES)


def _pad_rows_to_heads(x, n_heads, d):
    r = x.shape[0]
    x = x.reshape(r, n_heads, d)
    return jnp.pad(x, ((0, 0), (0, 0), (0, LANES - d))).reshape(r, n_heads * LANES)


def kernel(x_prompt, x_sample, cache_mla_ckv, cache_mla_kpe, cache_gqa_k, cache_gqa_v, cache_diff_k, cache_diff_v, c, c_ctx, g_mix, g_ffn, w_mod, b_mod, w_router, b_router, w_e1, w_e3, w_e2, mla_w_dq, mla_g_q, mla_w_uq, mla_w_dkv, mla_g_kv, mla_w_ukv, mla_g_qn, mla_g_kn, mla_w_o, gqa_w_qkv, gqa_g_qn, gqa_g_kn, gqa_w_o, diff_w_qkv, diff_g_qn, diff_g_kn, diff_lam_q1, diff_lam_k1, diff_lam_q2, diff_lam_k2, diff_g_sub, diff_w_o):
    d = D_MODEL
    x = (x_prompt.reshape(T_PROMPT, d), x_sample.reshape(T_SAMPLE, d))

    c_all = jnp.concatenate([c_ctx[None, :], c, jnp.zeros((MOD_ROWS - 1 - DEC_BATCH, d), F32)], axis=0)
    mod = _modulation(c_all, w_mod, b_mod).reshape(DEPTH * MOD_ROWS, 1, 6 * d)

    w_hi = w_router.astype(BF16)
    w_lo = (w_router - w_hi.astype(F32)).astype(BF16)
    pad_r = ((0, 0), (0, LANES - MOE_EXPERTS))
    wr = jnp.concatenate([jnp.pad(w_hi, pad_r), jnp.pad(w_lo, pad_r)], axis=1)
    br = _pad_vec(b_router)

    w1b, w3b, w2b = w_e1.astype(BF16), w_e3.astype(BF16), w_e2.astype(BF16)
    t_idx = jnp.arange(TM_MOE)
    tri = (t_idx[None, :] < t_idx[:, None]).astype(BF16)

    tab_mla = _rope_tables(MLA_ROPE, MLA_NOPE)
    tab_64 = _rope_tables(GQA_HD, 0)

    new_mla, new_gqa, new_diff = [], [], []
    for i in range(DEPTH):
        kind, j = i % 3, i // 3
        g_mix_i = g_mix[i].reshape(1, d)
        if kind == 0:
            w_down = jnp.concatenate(
                [mla_w_dq[j], mla_w_dkv[j],
                 jnp.zeros((d, LANES - MLA_ROPE), F32)], axis=1).astype(BF16)
            wuq = _pad_heads(mla_w_uq[j], MLA_HEADS, MLA_QK).astype(BF16)
            ukv = mla_w_ukv[j].reshape(MLA_KV_RANK, MLA_HEADS, MLA_NOPE + MLA_V)
            wk_nope = jnp.pad(ukv[:, :, :MLA_NOPE], ((0, 0), (0, 0), (0, LANES - MLA_NOPE)))
            place = jnp.zeros((LANES, MLA_HEADS, LANES), F32)
            r = jnp.arange(MLA_ROPE)
            place = place.at[r, :, MLA_NOPE + r].set(1.0)
            wk = jnp.concatenate([wk_nope.reshape(MLA_KV_RANK, -1), place.reshape(LANES, -1)],
                                 axis=0).astype(BF16)
            wv = ukv[:, :, MLA_NOPE:].reshape(MLA_KV_RANK, MLA_HEADS * MLA_V).astype(BF16)
            gqn = _pad_vec(mla_g_qn[j], LOG2E * MLA_QK ** -0.5)
            gkn = _pad_vec(mla_g_kn[j])
            q, k, v, ckv_c, kpe_c = _mla_proj(
                x, mod, g_mix_i, w_down, mla_g_q[j].reshape(1, -1), wuq,
                mla_g_kv[j].reshape(1, -1), wk, wv, _gain_t(gqn), _gain_t(gkn), tab_mla, i)
            cache_kpe = jnp.pad(cache_mla_kpe[:, j].reshape(DEC_BATCH * PAST_LEN, MLA_ROPE),
                                ((0, 0), (0, LANES - MLA_ROPE)))
            kc, vc = _mla_cache_expand(cache_mla_ckv[:, j].reshape(DEC_BATCH * PAST_LEN, MLA_KV_RANK),
                                       cache_kpe, wk, wv, _gain_t(gkn))
            o_p, o_s = _attention(q, k, v, kc, vc, (), mode="pair", k_shared=False, lam_init=0.0)
            w_o = mla_w_o[j].astype(BF16)
            new_mla.append((ckv_c[:T_PROMPT].reshape(BATCH, SEQ, MLA_KV_RANK),
                            kpe_c[:T_PROMPT, :MLA_ROPE].reshape(BATCH, SEQ, MLA_ROPE)))
        elif kind == 1:
            nq, nk = GQA_Q_HEADS, GQA_KV_HEADS
            wq_, wk_, wv_ = jnp.split(gqa_w_qkv[j], [nq * GQA_HD, (nq + nk) * GQA_HD], axis=1)
            wv_ = wv_.reshape(d, nk, GQA_HD)
            w = jnp.concatenate([_pad_heads(wq_, nq, GQA_HD), _pad_heads(wk_, nk, GQA_HD),
                                 jnp.concatenate([wv_, wv_], axis=-1).reshape(d, nk * LANES)],
                                axis=1).astype(BF16)
            gq, gk = _pad_vec(gqa_g_qn[j], LOG2E * GQA_HD ** -0.5), _pad_vec(gqa_g_kn[j])
            q, k, v, k_c, v_c = _qkv_proj(
                x, mod, g_mix_i, w, _gain_t(gq), _gain_t(gk), tab_64, i,
                nq=nq, nk=nk, nv=nk * LANES, d_real=GQA_HD, half=GQA_HD // 2)
            rows = DEC_BATCH * PAST_LEN
            kc = _pad_rows_to_heads(cache_gqa_k[:, j].reshape(rows, nk * GQA_HD), nk, GQA_HD).astype(BF16)
            cv = cache_gqa_v[:, j].reshape(rows, nk, GQA_HD)
            vc = jnp.concatenate([cv, cv], axis=-1).reshape(rows, nk * LANES).astype(BF16).T
            o_p, o_s = _attention(q, k, v, kc, vc, (), mode="pair", k_shared=True, lam_init=0.0)
            w_o = gqa_w_o[j].astype(BF16)
            new_gqa.append((k_c[:T_PROMPT].reshape(BATCH, SEQ, nk, LANES)[..., :GQA_HD],
                            v_c[:T_PROMPT].reshape(BATCH, SEQ, nk, LANES)[..., :GQA_HD]))
        else:
            nh = DIFF_HEADS
            lam_init = 0.8 - 0.6 * math.exp(-0.3 * i)
            wq_, wk_, wv_ = jnp.split(diff_w_qkv[j], 3, axis=1)
            w = jnp.concatenate([_pad_heads(wq_, 2 * nh, DIFF_HD), _pad_heads(wk_, 2 * nh, DIFF_HD),
                                 wv_], axis=1).astype(BF16)
            gq, gk = _pad_vec(diff_g_qn[j], LOG2E * DIFF_HD ** -0.5), _pad_vec(diff_g_kn[j])
            q, k, v, k_c, v_c = _qkv_proj(
                x, mod, g_mix_i, w, _gain_t(gq), _gain_t(gk), tab_64, i,
                nq=2 * nh, nk=2 * nh, nv=nh * 2 * DIFF_HD, d_real=DIFF_HD, half=DIFF_HD // 2)
            rows = DEC_BATCH * PAST_LEN
            kc = _pad_rows_to_heads(cache_diff_k[:, j].reshape(rows, 2 * nh * DIFF_HD), 2 * nh,
                                    DIFF_HD).astype(BF16)
            vc = cache_diff_v[:, j].reshape(rows, nh * 2 * DIFF_HD).astype(BF16).T
            lam_p = jnp.concatenate([_pad_vec(diff_lam_q1[j]), _pad_vec(diff_lam_k1[j]),
                                     _pad_vec(diff_lam_q2[j]), _pad_vec(diff_lam_k2[j])], axis=0)
            o_p, o_s = _attention(q, k, v, kc, vc, (lam_p, diff_g_sub[j].reshape(1, LANES)),
                                  mode="diff", k_shared=False, lam_init=lam_init)
            w_o = diff_w_o[j].astype(BF16)
            new_diff.append((k_c[:T_PROMPT].reshape(BATCH, SEQ, nh, 2, LANES)[..., :DIFF_HD],
                             v_c[:T_PROMPT].reshape(BATCH, SEQ, nh, 2 * DIFF_HD)))

        x, h2, comb = _oproj(o_p, o_s, x, mod, w_o, g_ffn[i].reshape(1, d), wr, br, i)
        w2g = w2b[i].reshape(MOE_GROUPS, EXPERTS_PER_GROUP * MOE_DIM, d)
        if i + 1 < DEPTH:
            x = _moe(h2, x, mod, comb, tri, w1b[i], w3b[i], w2g, i)
        else:
            y_prompt = _moe(h2, x, mod, comb, tri, w1b[i], w3b[i], w2g, i, 0, T_PROMPT)
            y_sample = _moe(h2, x, mod, comb, tri, w1b[i], w3b[i], w2g, i, T_PROMPT, T_SAMPLE)

    y_prompt = y_prompt.reshape(BATCH, SEQ, d)
    y_sample = y_sample.reshape(DEC_BATCH, DEC_SEQ, d)
    stack = lambda items, k: jnp.stack([t[k] for t in items], axis=1)
    return (y_prompt, y_sample, stack(new_mla, 0), stack(new_mla, 1), stack(new_gqa, 0),
            stack(new_gqa, 1), stack(new_diff, 0), stack(new_diff, 1))
```
